```python
import math
import jax, jax.numpy as jnp
from jax import lax
import numpy as np

D_MODEL = 1024
BATCH = 1
SEQ = 16384
DEPTH = 2
DEC_BATCH = 128
DEC_SEQ = 1
PAST_LEN = 16384
PAGE_SIZE = 128

HEAD_DIM = 64
NSA_HEADS = 6
CMP_BLOCK = 32
SEL_BLOCK = 64
TOP_N = 16
WINDOW = 512
MLA_HEADS = 6
MLA_Q_LORA = 192
MLA_KV_LORA = 128
MLA_NOPE = 64
MLA_ROPE = 32
MLA_V = 64
ROPE_THETA = 10000.0
DIFF_HEADS = 4
DIFF_KV_HEADS = 2
DIFF_QK = 32
DIFF_V = 64
REL_BUCKETS = 32
REL_MAX_DIST = 2048
BIAS_HEADS = NSA_HEADS + DIFF_HEADS
N_GROUPS = 4
EXPERTS_PER_GROUP = 4
N_EXPERTS = N_GROUPS * EXPERTS_PER_GROUP
TOP_K = 2
D_EXPERT = 256
Q_BLOCK = 128
FFN_BLOCK = 128
RMS_EPS = 1e-6

NSA_SCALE = HEAD_DIM ** -0.5
MLA_SCALE = (MLA_NOPE + MLA_ROPE) ** -0.5
DIFF_SCALE = DIFF_QK ** -0.5
MIX_WIDTH = NSA_HEADS * HEAD_DIM + MLA_HEADS * MLA_V + DIFF_HEADS * DIFF_V
NSA_ROW = 4 * HEAD_DIM
WIN_ROW = 2 * HEAD_DIM
MLA_ROW = MLA_KV_LORA + MLA_ROPE
DIFF_K_W = DIFF_KV_HEADS * 2 * DIFF_QK
DIFF_ROW = DIFF_K_W + DIFF_KV_HEADS * DIFF_V
IN_SIZES = (NSA_HEADS * HEAD_DIM, 6 * HEAD_DIM, NSA_HEADS * 3,
            MLA_Q_LORA, MLA_KV_LORA, MLA_ROPE,
            DIFF_HEADS * 2 * DIFF_QK, DIFF_K_W, DIFF_KV_HEADS * DIFF_V)
IN_WIDTH = sum(IN_SIZES)

kernel_name = 'hymba_nsa_mla_diff_hmoe_step'


def split_points(sizes):
    pts, acc = [], 0
    for s in sizes[:-1]:
        acc += s
        pts.append(acc)
    return pts


def rms_norm(x, gain):
    xf = x.astype(jnp.float32)
    y = xf * lax.rsqrt(jnp.mean(xf * xf, axis=-1, keepdims=True) + RMS_EPS)
    return (y * gain.astype(jnp.float32)).astype(x.dtype)


def masked_softmax(logits, mask):
    s = jnp.where(mask, logits, -jnp.inf)
    m = jnp.max(s, axis=-1, keepdims=True)
    m = jnp.where(jnp.isfinite(m), m, 0.0)
    e = jnp.exp(s - m)
    z = jnp.sum(e, axis=-1, keepdims=True)
    return e / jnp.where(z > 0.0, z, 1.0)


def rope(x, pos):
    half = x.shape[-1] // 2
    inv = ROPE_THETA ** (-jnp.arange(half, dtype=jnp.float32) / half)
    ang = pos.astype(jnp.float32)[:, None] * inv[None, :]
    cos, sin = jnp.cos(ang)[:, None, :], jnp.sin(ang)[:, None, :]
    xf = x.astype(jnp.float32)
    x1, x2 = xf[..., :half], xf[..., half:]
    return jnp.concatenate([x1 * cos - x2 * sin, x1 * sin + x2 * cos], axis=-1).astype(x.dtype)


def rel_bucket(dist):
    n = jnp.maximum(dist, 0)
    max_exact = REL_BUCKETS // 2
    nf = jnp.maximum(n, 1).astype(jnp.float32)
    large = max_exact + (jnp.log(nf / max_exact) / math.log(REL_MAX_DIST / max_exact)
                         * (REL_BUCKETS - max_exact)).astype(jnp.int32)
    large = jnp.minimum(large, REL_BUCKETS - 1)
    return jnp.where(n < max_exact, n, large)


def rel_bias(table, dist):
    return jnp.take(table.astype(jnp.float32), rel_bucket(dist), axis=0)


def project_tokens(h, pos, w_in, nsa_gain, mla_q_norm, mla_kv_norm, w_uq, nope_gain, rope_gain, diff_qk_gain):
    B, T, _ = h.shape
    q_n, kv_n, g_n, c_q, c_kv, k_r, q_d, k_d, v_d = jnp.split(h @ w_in, split_points(IN_SIZES), axis=-1)
    q_nsa = rms_norm(q_n.reshape(B, T, NSA_HEADS, HEAD_DIM), nsa_gain[0])
    kc_r, vc_r, ks, vs, kw, vw = jnp.split(kv_n, 6, axis=-1)
    gates = jax.nn.sigmoid(g_n.astype(jnp.float32)).reshape(B, T, NSA_HEADS, 3).astype(h.dtype)
    nsa_row = jnp.concatenate([kc_r, vc_r, rms_norm(ks, nsa_gain[2]), vs], axis=-1)
    win_row = jnp.concatenate([rms_norm(kw, nsa_gain[3]), vw], axis=-1)
    qf = (rms_norm(c_q, mla_q_norm) @ w_uq).reshape(B, T, MLA_HEADS, MLA_NOPE + MLA_ROPE)
    qn = rms_norm(qf[..., :MLA_NOPE], nope_gain[0])
    qr = rope(rms_norm(qf[..., MLA_NOPE:], rope_gain[0]), pos)
    kr = rope(rms_norm(k_r, rope_gain[1])[:, :, None, :], pos)[:, :, 0, :]
    mla_row = jnp.concatenate([rms_norm(c_kv, mla_kv_norm), kr], axis=-1)
    qd = rms_norm(q_d.reshape(B, T, DIFF_HEADS, 2, DIFF_QK), diff_qk_gain[:2])
    kd = rms_norm(k_d.reshape(B, T, DIFF_KV_HEADS, 2, DIFF_QK), diff_qk_gain[2:])
    diff_row = jnp.concatenate([kd.reshape(B, T, DIFF_K_W), v_d], axis=-1)
    return (q_nsa, gates, nsa_row, win_row, qn, qr, mla_row, qd, diff_row)


def nsa_compress(k_raw, v_raw, pe, w_cmp, gain):
    B, L, Dh = k_raw.shape
    nc = L // CMP_BLOCK

    def pool_blocks(a, pe_j, w):
        a = a[:, :nc * CMP_BLOCK].reshape(B, nc, CMP_BLOCK, Dh) + pe_j
        return jnp.mean(a.astype(jnp.float32), axis=2).astype(a.dtype) @ w

    kc = rms_norm(pool_blocks(k_raw, pe[0], w_cmp[0]), gain)
    vc = pool_blocks(v_raw, pe[1], w_cmp[1])
    cend = (jnp.arange(nc) + 1) * CMP_BLOCK - 1
    return kc, vc, cend


def to_sel_blocks(a):
    B, L, D = a.shape
    ns = -(-L // SEL_BLOCK)
    a = jnp.pad(a, ((0, 0), (0, ns * SEL_BLOCK - L), (0, 0)))
    return a.reshape(B, ns, SEL_BLOCK, D)


def nsa_attend(q, gates, qpos, kc, vc, cend, ks_b, vs_b, kw, vw, wpos, table):
    B, Tq, H, Dh = q.shape
    tab = table[:, :NSA_HEADS]
    dist_c = qpos[:, None] - cend[None, :]
    s_c = jnp.einsum('bqhd,bcd->bhqc', q, kc).astype(jnp.float32) * NSA_SCALE
    s_c = s_c + jnp.moveaxis(rel_bias(tab, dist_c), -1, 0)
    p_c = masked_softmax(s_c, dist_c >= 0)
    o_c = jnp.einsum('bhqc,bcd->bqhd', p_c.astype(vc.dtype), vc)
    nc, ns = kc.shape[1], ks_b.shape[1]
    ratio = SEL_BLOCK // CMP_BLOCK
    imp = jnp.pad(jnp.sum(p_c, axis=1), ((0, 0), (0, 0), (0, ns * ratio - nc)))
    imp = imp.reshape(B, Tq, ns, ratio).sum(-1)
    blk = jnp.arange(ns)[None, :]
    cur = (qpos // SEL_BLOCK)[:, None]
    forced = (blk == 0) | (blk == cur) | (blk == cur - 1)
    valid = blk * SEL_BLOCK <= qpos[:, None]
    score = jnp.where(valid, jnp.where(forced, jnp.inf, imp), -jnp.inf)
    _, idx = lax.top_k(score, min(TOP_N, ns))
    n = idx.shape[-1]
    take = jax.vmap(lambda blocks, ix: blocks[ix])
    k_s, v_s = take(ks_b, idx), take(vs_b, idx)
    spos = idx[..., None] * SEL_BLOCK + jnp.arange(SEL_BLOCK)
    dist_s = qpos[None, :, None, None] - spos
    s_s = jnp.einsum('bqhd,bqnkd->bhqnk', q, k_s).astype(jnp.float32) * NSA_SCALE
    s_s = s_s + jnp.moveaxis(rel_bias(tab, dist_s), -1, 1)
    p_s = masked_softmax(s_s.reshape(B, H, Tq, n * SEL_BLOCK),
                         (dist_s >= 0).reshape(B, 1, Tq, n * SEL_BLOCK))
    o_s = jnp.einsum('bhqk,bqkd->bqhd', p_s.astype(v_s.dtype), v_s.reshape(B, Tq, n * SEL_BLOCK, Dh))
    dist_w = qpos[:, None] - wpos[None, :]
    mask_w = (dist_w >= 0) & (dist_w < WINDOW) & (wpos >= 0)[None, :]
    s_w = jnp.einsum('bqhd,bkd->bhqk', q, kw).astype(jnp.float32) * NSA_SCALE
    s_w = s_w + jnp.moveaxis(rel_bias(tab, dist_w), -1, 0)
    p_w = masked_softmax(s_w, mask_w)
    o_w = jnp.einsum('bhqk,bkd->bqhd', p_w.astype(vw.dtype), vw)
    return gates[..., 0:1] * o_c + gates[..., 1:2] * o_s + gates[..., 2:3] * o_w


def mla_keys(rows, w_ukv, nope_gain):
    B, L, _ = rows.shape
    kv = (rows[..., :MLA_KV_LORA] @ w_ukv).reshape(B, L, MLA_HEADS, MLA_NOPE + MLA_V)
    return rms_norm(kv[..., :MLA_NOPE], nope_gain[1]), rows[..., MLA_KV_LORA:], kv[..., MLA_NOPE:]


def mla_attend(qn, qr, kn, kr, v, qpos, kpos):
    s = (jnp.einsum('bqhd,bkhd->bhqk', qn, kn)
         + jnp.einsum('bqhr,bkr->bhqk', qr, kr)).astype(jnp.float32) * MLA_SCALE
    p = masked_softmax(s, kpos[None, :] <= qpos[:, None])
    return jnp.einsum('bhqk,bkhd->bqhd', p.astype(v.dtype), v)


def diff_unpack(rows):
    B, L, _ = rows.shape
    k = rows[..., :DIFF_K_W].reshape(B, L, DIFF_KV_HEADS, 2, DIFF_QK)
    v = rows[..., DIFF_K_W:].reshape(B, L, DIFF_KV_HEADS, DIFF_V)
    return k, v


def diff_attend(q, k, v, qpos, kpos, table, lam, lam_init, head_gain):
    B, Tq = q.shape[:2]
    J = DIFF_HEADS // DIFF_KV_HEADS
    qg = q.reshape(B, Tq, DIFF_KV_HEADS, J, 2, DIFF_QK)
    s = jnp.einsum('bqgjmd,bkgmd->bgjmqk', qg, k).astype(jnp.float32) * DIFF_SCALE
    dist = qpos[:, None] - kpos[None, :]
    bias = jnp.moveaxis(rel_bias(table[:, NSA_HEADS:], dist), -1, 0)
    bias = bias.reshape(DIFF_KV_HEADS, J, 1, Tq, dist.shape[1])
    p = masked_softmax(s + bias, dist >= 0)
    a = p[:, :, :, 0] - lam * p[:, :, :, 1]
    o = jnp.einsum('bgjqk,bkgd->bqgjd', a.astype(v.dtype), v).reshape(B, Tq, DIFF_HEADS, DIFF_V)
    return rms_norm(o, head_gain) * (1.0 - lam_init)


def merge_groups(o_a, o_b, o_c):
    B, T = o_a.shape[:2]
    return jnp.concatenate([o_a.reshape(B, T, -1), o_b.reshape(B, T, -1), o_c.reshape(B, T, -1)], axis=-1)


def prompt_mixers(pieces, pe_cmp, w_cmp, nsa_gain, w_ukv, nope_gain, diff_head_gain, lam, lam_init, table):
    q_nsa, gates, nsa_row, win_row, qn, qr, mla_row, qd, diff_row = pieces
    B, S = q_nsa.shape[:2]
    kc_r, vc_r, ks, vs = jnp.split(nsa_row, 4, axis=-1)
    kc, vc, cend = nsa_compress(kc_r, vc_r, pe_cmp, w_cmp, nsa_gain[1])
    ks_b, vs_b = to_sel_blocks(ks), to_sel_blocks(vs)
    kw_pad = jnp.pad(win_row, ((0, 0), (WINDOW, 0), (0, 0)))
    kn, kr, v_m = mla_keys(mla_row, w_ukv, nope_gain)
    kd, vd = diff_unpack(diff_row)
    kpos = jnp.arange(S)
    nqb = S // Q_BLOCK

    def blk(a):
        return jnp.swapaxes(a.reshape(B, nqb, Q_BLOCK, *a.shape[2:]), 0, 1)

    def body(args):
        qb, q1, g1, qn1, qr1, qd1 = args
        qpos = qb * Q_BLOCK + jnp.arange(Q_BLOCK)
        wrows = lax.dynamic_slice_in_dim(kw_pad, qb * Q_BLOCK, WINDOW + Q_BLOCK, axis=1)
        wpos = qb * Q_BLOCK - WINDOW + jnp.arange(WINDOW + Q_BLOCK)
        o_a = nsa_attend(q1, g1, qpos, kc, vc, cend, ks_b, vs_b,
                         wrows[..., :HEAD_DIM], wrows[..., HEAD_DIM:], wpos, table)
        o_b = mla_attend(qn1, qr1, kn, kr, v_m, qpos, kpos)
        o_c = diff_attend(qd1, kd, vd, qpos, kpos, table, lam, lam_init, diff_head_gain)
        return merge_groups(o_a, o_b, o_c)

    out = lax.map(body, (jnp.arange(nqb), blk(q_nsa), blk(gates), blk(qn), blk(qr), blk(qd)))
    return jnp.swapaxes(out, 0, 1).reshape(B, S, MIX_WIDTH)


def sample_mixers(pieces, win_buf, page_table, pool_nsa, pool_mla, pool_diff, l,
                  pe_cmp, w_cmp, nsa_gain, w_ukv, nope_gain, diff_head_gain, lam, lam_init, table):
    q_nsa, gates, nsa_row, win_row, qn, qr, mla_row, qd, diff_row = pieces
    T = q_nsa.shape[1]
    wb = win_buf.shape[1]
    qpos = PAST_LEN + jnp.arange(T)
    kpos = jnp.arange(PAST_LEN + T)
    wpos = PAST_LEN - wb + jnp.arange(wb + T)

    def with_past(pool, pt, new):
        past = pool[l, pt].reshape(-1, pool.shape[-1])
        return jnp.concatenate([past, new], axis=0)[None]

    def body(args):
        pt, q1, g1, nr, wr, qn1, qr1, mr, qd1, dr, wbuf = args
        kc_r, vc_r, ks, vs = jnp.split(with_past(pool_nsa, pt, nr), 4, axis=-1)
        kc, vc, cend = nsa_compress(kc_r, vc_r, pe_cmp, w_cmp, nsa_gain[1])
        wrows = jnp.concatenate([wbuf, wr], axis=0)[None]
        o_a = nsa_attend(q1[None], g1[None], qpos, kc, vc, cend, to_sel_blocks(ks), to_sel_blocks(vs),
                         wrows[..., :HEAD_DIM], wrows[..., HEAD_DIM:], wpos, table)
        kn, kr, v_m = mla_keys(with_past(pool_mla, pt, mr), w_ukv, nope_gain)
        o_b = mla_attend(qn1[None], qr1[None], kn, kr, v_m, qpos, kpos)
        kd, vd = diff_unpack(with_past(pool_diff, pt, dr))
        o_c = diff_attend(qd1[None], kd, vd, qpos, kpos, table, lam, lam_init, diff_head_gain)
        return merge_groups(o_a, o_b, o_c)[0]

    return lax.map(body, (page_table, q_nsa, gates, nsa_row, win_row, qn, qr, mla_row, qd, diff_row, win_buf))


def moe_tokens(h, w_group, b_group, w_router, b_router, w1, w3, w2):
    gl = jnp.einsum('td,dg->tg', h, w_group).astype(jnp.float32) + b_group.astype(jnp.float32)
    g = jnp.argmax(gl, axis=-1)
    p_g = jnp.take_along_axis(jax.nn.softmax(gl, axis=-1), g[:, None], axis=-1)
    el = jnp.einsum('td,gde->tge', h, w_router).astype(jnp.float32) + b_router.astype(jnp.float32)
    el = jnp.take_along_axis(el, g[:, None, None], axis=1)[:, 0]
    top_v, top_i = lax.top_k(el, TOP_K)
    w = jax.nn.softmax(top_v, axis=-1) * p_g
    eid = g[:, None] * EXPERTS_PER_GROUP + top_i
    comb = jnp.einsum('tk,tke->te', w, jax.nn.one_hot(eid, N_EXPERTS, dtype=jnp.float32))
    a = jnp.einsum('td,edf->tef', h, w1)
    b = jnp.einsum('td,edf->tef', h, w3)
    act = jax.nn.silu(a) * b * comb[:, :, None].astype(h.dtype)
    return jnp.einsum('tef,efd->td', act, w2)


def moe_blocked(h, moe_w):
    B, S, D = h.shape
    out = lax.map(lambda t: moe_tokens(t, *moe_w), h.reshape(-1, FFN_BLOCK, D))
    return out.reshape(B, S, D)


def setup_inputs(seed: int = 0) -> dict:
    ks = jax.random.split(jax.random.key(seed), 64)
    counter = [0]

    def nxt():
        counter[0] += 1
        return ks[counter[0] - 1]

    def nrm(shape, scale=1.0):
        return jax.random.normal(nxt(), shape, jnp.float32) * scale

    def gain(shape):
        return 1.0 + nrm(shape, 0.01)

    n_pages = PAST_LEN // PAGE_SIZE
    n_used = DEC_BATCH * n_pages
    n_pool = n_used + n_used // 4
    wb = min(WINDOW, PAST_LEN)
    x_prompt = nrm((BATCH, SEQ, D_MODEL))
    x_sample = nrm((DEC_BATCH, DEC_SEQ, D_MODEL))
    cache_nsa = nrm((DEPTH, n_pool, PAGE_SIZE, NSA_ROW))
    cache_mla = nrm((DEPTH, n_pool, PAGE_SIZE, MLA_ROW))
    cache_diff = nrm((DEPTH, n_pool, PAGE_SIZE, DIFF_ROW))
    state_nsa_win = nrm((DEPTH, DEC_BATCH, wb, WIN_ROW))
    page_table = jax.random.permutation(nxt(), n_pool)[:n_used].reshape(DEC_BATCH, n_pages).astype(jnp.int32)
    return {
        'x_prompt': x_prompt,
        'x_sample': x_sample,
        'cache_nsa': cache_nsa,
        'cache_mla': cache_mla,
        'cache_diff': cache_diff,
        'state_nsa_win': state_nsa_win,
        'page_table': page_table,
        'attn_norm': gain((DEPTH, D_MODEL)),
        'w_in': nrm((DEPTH, D_MODEL, IN_WIDTH), D_MODEL ** -0.5),
        'nsa_qk_gain': gain((DEPTH, 4, HEAD_DIM)),
        'nsa_cmp_pe': nrm((DEPTH, 2, CMP_BLOCK, HEAD_DIM), 0.3),
        'nsa_cmp_w': nrm((DEPTH, 2, HEAD_DIM, HEAD_DIM), HEAD_DIM ** -0.5),
        'mla_q_norm': gain((DEPTH, MLA_Q_LORA)),
        'mla_kv_norm': gain((DEPTH, MLA_KV_LORA)),
        'mla_w_uq': nrm((DEPTH, MLA_Q_LORA, MLA_HEADS * (MLA_NOPE + MLA_ROPE)), MLA_Q_LORA ** -0.5),
        'mla_w_ukv': nrm((DEPTH, MLA_KV_LORA, MLA_HEADS * (MLA_NOPE + MLA_V)), MLA_KV_LORA ** -0.5),
        'mla_nope_gain': gain((DEPTH, 2, MLA_NOPE)),
        'mla_rope_gain': gain((DEPTH, 2, MLA_ROPE)),
        'diff_qk_gain': gain((DEPTH, 4, DIFF_QK)),
        'diff_lambda': nrm((DEPTH, 4, DIFF_QK), 0.1),
        'diff_head_gain': gain((DEPTH, DIFF_V)),
        'w_out': nrm((DEPTH, MIX_WIDTH, D_MODEL), MIX_WIDTH ** -0.5),
        'rel_bias_table': nrm((REL_BUCKETS, BIAS_HEADS), 0.5),
        'ffn_norm': gain((DEPTH, D_MODEL)),
        'moe_w_group': nrm((DEPTH, D_MODEL, N_GROUPS), D_MODEL ** -0.5),
        'moe_b_group': nrm((DEPTH, N_GROUPS), 0.01),
        'moe_w_router': nrm((DEPTH, N_GROUPS, D_MODEL, EXPERTS_PER_GROUP), D_MODEL ** -0.5),
        'moe_b_router': nrm((DEPTH, N_GROUPS, EXPERTS_PER_GROUP), 0.01),
        'moe_w1': nrm((DEPTH, N_EXPERTS, D_MODEL, D_EXPERT), D_MODEL ** -0.5),
        'moe_w3': nrm((DEPTH, N_EXPERTS, D_MODEL, D_EXPERT), D_MODEL ** -0.5),
        'moe_w2': nrm((DEPTH, N_EXPERTS, D_EXPERT, D_MODEL), D_EXPERT ** -0.5),
    }


def reference(x_prompt, x_sample, cache_nsa, cache_mla, cache_diff, state_nsa_win, page_table,
              attn_norm, w_in, nsa_qk_gain, nsa_cmp_pe, nsa_cmp_w, mla_q_norm, mla_kv_norm,
              mla_w_uq, mla_w_ukv, mla_nope_gain, mla_rope_gain, diff_qk_gain, diff_lambda,
              diff_head_gain, w_out, rel_bias_table, ffn_norm, moe_w_group, moe_b_group,
              moe_w_router, moe_b_router, moe_w1, moe_w3, moe_w2):
    xp, xs = x_prompt, x_sample
    S = xp.shape[1]
    pos_p = jnp.arange(S)
    pos_s = PAST_LEN + jnp.arange(xs.shape[1])
    nsa_p, nsa_s, mla_p, mla_s, diff_p, diff_s, win_p, win_s = [], [], [], [], [], [], [], []
    for l in range(DEPTH):
        lam_init = 0.8 - 0.6 * math.exp(-0.3 * l)
        lv = diff_lambda[l].astype(jnp.float32)
        lam = jnp.exp(jnp.sum(lv[0] * lv[1])) - jnp.exp(jnp.sum(lv[2] * lv[3])) + lam_init
        proj_w = (w_in[l], nsa_qk_gain[l], mla_q_norm[l], mla_kv_norm[l], mla_w_uq[l],
                  mla_nope_gain[l], mla_rope_gain[l], diff_qk_gain[l])
        key_w = (nsa_cmp_pe[l], nsa_cmp_w[l], nsa_qk_gain[l], mla_w_ukv[l], mla_nope_gain[l],
                 diff_head_gain[l], lam, lam_init, rel_bias_table)
        moe_w = (moe_w_group[l], moe_b_group[l], moe_w_router[l], moe_b_router[l],
                 moe_w1[l], moe_w3[l], moe_w2[l])
        pp = project_tokens(rms_norm(xp, attn_norm[l]), pos_p, *proj_w)
        xp = xp + prompt_mixers(pp, *key_w) @ w_out[l]
        xp = xp + moe_blocked(rms_norm(xp, ffn_norm[l]), moe_w)
        ps = project_tokens(rms_norm(xs, attn_norm[l]), pos_s, *proj_w)
        win_buf = state_nsa_win[l]
        xs = xs + sample_mixers(ps, win_buf, page_table, cache_nsa, cache_mla, cache_diff, l, *key_w) @ w_out[l]
        hs = rms_norm(xs, ffn_norm[l])
        xs = xs + moe_tokens(hs.reshape(-1, D_MODEL), *moe_w).reshape(xs.shape)
        nsa_p.append(pp[2]); mla_p.append(pp[6]); diff_p.append(pp[8])
        win_p.append(pp[3][:, S - min(WINDOW, S):])
        nsa_s.append(ps[2]); mla_s.append(ps[6]); diff_s.append(ps[8])
        win_s.append(jnp.concatenate([win_buf, ps[3]], axis=1)[:, -win_buf.shape[1]:])
    return (xp, xs, jnp.stack(nsa_p), jnp.stack(nsa_s), jnp.stack(mla_p), jnp.stack(mla_s),
            jnp.stack(diff_p), jnp.stack(diff_s), jnp.stack(win_p), jnp.stack(win_s))
```

```python
import functools
import math

import numpy as np
import jax
import jax.numpy as jnp
from jax import lax
from jax.experimental import pallas as pl
from jax.experimental.pallas import tpu as pltpu

F32 = jnp.float32
BF16 = jnp.bfloat16
HIGHEST = lax.Precision.HIGHEST

D_MODEL = 1024
PAST_LEN = 16384
PAGE_SIZE = 128
HEAD_DIM = 64
NSA_HEADS = 6
CMP_BLOCK = 32
SEL_BLOCK = 64
TOP_N = 16
WINDOW = 512
MLA_HEADS = 6
MLA_Q_LORA = 192
MLA_KV_LORA = 128
MLA_NOPE = 64
MLA_ROPE = 32
MLA_V = 64
ROPE_THETA = 10000.0
DIFF_HEADS = 4
DIFF_KV_HEADS = 2
DIFF_QK = 32
DIFF_V = 64
REL_BUCKETS = 32
REL_MAX_DIST = 2048
N_GROUPS = 4
EXPERTS_PER_GROUP = 4
N_EXPERTS = N_GROUPS * EXPERTS_PER_GROUP
D_EXPERT = 256
RMS_EPS = 1e-6
NSA_SCALE = HEAD_DIM ** -0.5
MLA_SCALE = (MLA_NOPE + MLA_ROPE) ** -0.5
DIFF_SCALE = DIFF_QK ** -0.5
IN_SIZES = (384, 384, 18, 192, 128, 32, 256, 128, 128)
IN_WIDTH = sum(IN_SIZES)

LANE = 128
VMEM_LIMIT = 56 * 1024 * 1024

OFF_A, OFF_B, OFF_C, OFF_D, OFF_E, OFF_F, OFF_G, OFF_H = 0, 768, 1152, 1408, 1536, 2560, 2688, 2816
W_Y = 2944
N_GRP = 128
G_NSA, G_QN, G_KVN, G_DQK, G_RG, G_NG, G_ONE, G_ZERO = 0, 256, 448, 576, 704, 768, 896, 897
ROPE_LO = 64


def _layout_stage1():
    src = np.full((W_Y,), IN_WIDTH, np.int32)
    grp = np.full((W_Y,), -1, np.int32)
    gsrc = np.full((W_Y,), G_ZERO, np.int32)
    pas = np.zeros((W_Y,), np.float32)
    gsize = np.ones((N_GRP,), np.float32)
    o_q, o_kv, o_g, o_cq, o_ckv, o_kr, o_qd, o_kd, o_vd = np.cumsum((0,) + IN_SIZES[:-1])
    for h in range(NSA_HEADS):
        for d in range(HEAD_DIM):
            c = OFF_A + h * LANE + d
            src[c], grp[c], gsrc[c] = o_q + h * HEAD_DIM + d, h, G_NSA + d
        gsize[h] = HEAD_DIM
    for i in range(384):
        c = OFF_B + i
        src[c] = o_kv + i
        blk = i // HEAD_DIM
        if blk == 2:
            grp[c], gsrc[c] = 6, G_NSA + 2 * HEAD_DIM + i % HEAD_DIM
        elif blk == 4:
            grp[c], gsrc[c] = 7, G_NSA + 3 * HEAD_DIM + i % HEAD_DIM
        else:
            pas[c] = 1.0
    gsize[6] = gsize[7] = HEAD_DIM
    for i in range(MLA_Q_LORA):
        c = OFF_C + i
        src[c], grp[c], gsrc[c] = o_cq + i, 8, G_QN + i
    gsize[8] = MLA_Q_LORA
    for i in range(MLA_KV_LORA):
        c = OFF_D + i
        src[c], grp[c], gsrc[c] = o_ckv + i, 9, G_KVN + i
    gsize[9] = MLA_KV_LORA
    for g in range(2):
        for j in range(2):
            for m in range(2):
                r = g * 4 + j * 2 + m
                for d in range(DIFF_QK):
                    c = OFF_E + r * LANE + g * 64 + m * 32 + d
                    src[c] = o_qd + (g * 2 + j) * 64 + m * 32 + d
                    grp[c], gsrc[c] = 10 + r, G_DQK + m * 32 + d
                gsize[10 + r] = DIFF_QK
    for i in range(128):
        c = OFF_F + i
        gm = i // 32
        src[c], grp[c], gsrc[c] = o_kd + i, 18 + gm, G_DQK + (2 + gm % 2) * 32 + i % 32
        gsize[18 + gm] = DIFF_QK
    for i in range(128):
        c = OFF_G + i
        src[c], pas[c] = o_vd + i, 1.0
    for i in range(18):
        c = OFF_H + i
        src[c], pas[c] = o_g + i, 1.0
    for d in range(MLA_ROPE):
        c = OFF_H + ROPE_LO + d
        src[c], grp[c], gsrc[c] = o_kr + d, 22, G_RG + MLA_ROPE + d
    gsize[22] = MLA_ROPE
    return src, grp, gsrc, pas, gsize


def _layout_q_mla():
    w = MLA_HEADS * LANE
    src = np.full((w,), MLA_HEADS * (MLA_NOPE + MLA_ROPE), np.int32)
    grp = np.full((w,), -1, np.int32)
    gsrc = np.full((w,), G_ZERO, np.int32)
    gsize = np.ones((N_GRP,), np.float32)
    for h in range(MLA_HEADS):
        for t in range(MLA_NOPE + MLA_ROPE):
            c = h * LANE + t
            src[c] = h * (MLA_NOPE + MLA_ROPE) + t
            if t < MLA_NOPE:
                grp[c], gsrc[c] = 2 * h, G_NG + t
            else:
                grp[c], gsrc[c] = 2 * h + 1, G_RG + (t - MLA_NOPE)
        gsize[2 * h], gsize[2 * h + 1] = MLA_NOPE, MLA_ROPE
    return src, grp, gsrc, gsize


def _layout_kv_mla():
    wk = MLA_HEADS * LANE
    wv = MLA_HEADS * MLA_V
    src = np.full((wk + wv,), MLA_HEADS * (MLA_NOPE + MLA_V), np.int32)
    grp = np.full((wk,), -1, np.int32)
    gsrc = np.full((wk,), G_ZERO, np.int32)
    gsize = np.ones((N_GRP,), np.float32)
    for h in range(MLA_HEADS):
        for d in range(MLA_NOPE):
            c = h * LANE + d
            src[c], grp[c], gsrc[c] = h * (MLA_NOPE + MLA_V) + d, h, G_NG + MLA_NOPE + d
        gsize[h] = MLA_NOPE
        for d in range(MLA_V):
            src[wk + h * MLA_V + d] = h * (MLA_NOPE + MLA_V) + MLA_NOPE + d
    return src, grp, gsrc, gsize


def _group_mats(grp, gsize):
    w = grp.shape[0]
    g = np.zeros((w, N_GRP), np.float32)
    cols = np.nonzero(grp >= 0)[0]
    g[cols, grp[cols]] = 1.0
    return g, np.ascontiguousarray(g.T), (1.0 / gsize).reshape(1, N_GRP).astype(np.float32)


_S1 = _layout_stage1()
_S2 = _layout_q_mla()
_S3 = _layout_kv_mla()
_G1, _GT1, _INVN1 = _group_mats(_S1[1], _S1[4])
_G2, _GT2, _INVN2 = _group_mats(_S2[1], _S2[3])
_G3, _GT3, _INVN3 = _group_mats(_S3[1], _S3[3])


def _take_cols(w, src):
    wz = jnp.concatenate([w, jnp.zeros((w.shape[0], 1), w.dtype)], axis=1)
    return jnp.take(wz, jnp.asarray(src), axis=1)


def _flat_gains(nsa_gain, q_norm, kv_norm, dqk_gain, rope_gain, nope_gain):
    return jnp.concatenate([nsa_gain.reshape(-1), q_norm, kv_norm, dqk_gain.reshape(-1),
                            rope_gain.reshape(-1), nope_gain.reshape(-1),
                            jnp.ones((1,), F32), jnp.zeros((1,), F32)]).astype(F32)


def _rope_tables(pos):
    half = MLA_ROPE // 2
    inv = ROPE_THETA ** (-jnp.arange(half, dtype=jnp.float32) / half)
    ang = pos.astype(jnp.float32)[:, None] * inv[None, :]
    cos, sin = jnp.cos(ang), jnp.sin(ang)
    t = pos.shape[0]
    one = jnp.ones((t, ROPE_LO), F32)
    z16 = jnp.zeros((t, half), F32)
    z32 = jnp.zeros((t, LANE - ROPE_LO - MLA_ROPE), F32)
    z64 = jnp.zeros((t, ROPE_LO), F32)
    c_tab = jnp.concatenate([one, cos, cos, z32], axis=1)
    s1_tab = jnp.concatenate([z64, -sin, z16, z32], axis=1)
    s2_tab = jnp.concatenate([z64, z16, sin, z32], axis=1)
    return c_tab, s1_tab, s2_tab


def _mm(a, b, exact):
    if exact:
        return jnp.dot(a, b, precision=HIGHEST, preferred_element_type=F32)
    return jnp.dot(a.astype(BF16), b.astype(BF16), preferred_element_type=F32)


def _mm_nt(a, b, exact):
    dn = (((1,), (1,)), ((), ()))
    if exact:
        return lax.dot_general(a, b, dn, precision=HIGHEST, preferred_element_type=F32)
    return lax.dot_general(a.astype(BF16), b.astype(BF16), dn, preferred_element_type=F32)


def _segnorm(y, g_ref, gt_ref, invn_ref, gain_ref, pass_row):
    ssq = jnp.dot(y * y, g_ref[...], precision=HIGHEST, preferred_element_type=F32)
    inv = lax.rsqrt(ssq * invn_ref[...] + RMS_EPS)
    fac = jnp.dot(inv, gt_ref[...], precision=HIGHEST, preferred_element_type=F32) * gain_ref[...]
    if pass_row is not None:
        fac = fac + pass_row
    return y * fac


def _rope_slab(x, c_tab, s1_tab, s2_tab):
    return (x * c_tab + pltpu.roll(x, LANE - MLA_ROPE // 2, 1) * s1_tab
            + pltpu.roll(x, MLA_ROPE // 2, 1) * s2_tab)


def _proj_kernel(x_ref, an_ref, w1_ref, g1_ref, gt1_ref, invn1_ref, gain1_ref, pass1_ref,
                 rc_ref, rs1_ref, rs2_ref, wuq_ref, g2_ref, gt2_ref, invn2_ref, gain2_ref,
                 wukv_ref, g3_ref, gt3_ref, invn3_ref, gain3_ref,
                 pool_ref, pem_ref, wc_ref, cgain_ref,
                 qnsa_ref, gates_ref, nsarow_ref, winrow_ref, qmla_ref, ckv_ref, kr_ref,
                 qd_ref, diffrow_ref, kmla_ref, vmla_ref, nsarowb_ref, winrowb_ref, diffrowb_ref,
                 cmp_ref, *, exact):
    x = x_ref[...]
    h = x * lax.rsqrt(jnp.mean(x * x, axis=-1, keepdims=True) + RMS_EPS) * an_ref[...]
    y = _mm(h, w1_ref[...], exact)
    yn = _segnorm(y, g1_ref, gt1_ref, invn1_ref, gain1_ref, pass1_ref[...])
    rc, rs1, rs2 = rc_ref[...], rs1_ref[...], rs2_ref[...]

    qnsa_ref[...] = (yn[:, OFF_A:OFF_A + 768] * NSA_SCALE).astype(qnsa_ref.dtype)
    nsa_row = yn[:, OFF_B:OFF_B + 256]
    nsarow_ref[...] = nsa_row
    nsarowb_ref[...] = nsa_row.astype(BF16)
    win_row = yn[:, OFF_B + 256:OFF_B + 384]
    winrow_ref[...] = win_row
    winrowb_ref[...] = win_row.astype(BF16)
    ckv = yn[:, OFF_D:OFF_D + 128]
    ckv_ref[...] = ckv
    qd_ref[...] = (yn[:, OFF_E:OFF_E + 1024] * DIFF_SCALE).astype(qd_ref.dtype)
    diff_row = yn[:, OFF_F:OFF_F + 256]
    diffrow_ref[...] = diff_row
    diffrowb_ref[...] = diff_row.astype(BF16)
    y_h = y[:, OFF_H:OFF_H + 128]
    gates_ref[...] = jax.nn.sigmoid(y_h)
    kr = _rope_slab(yn[:, OFF_H:OFF_H + 128], rc, rs1, rs2)
    lane = lax.broadcasted_iota(jnp.int32, kr.shape, 1)
    kr = jnp.where((lane >= ROPE_LO) & (lane < ROPE_LO + MLA_ROPE), kr, 0.0)
    kr_ref[...] = kr

    qf = _mm(yn[:, OFF_C:OFF_C + 256], wuq_ref[...], exact)
    qf = _segnorm(qf, g2_ref, gt2_ref, invn2_ref, gain2_ref, None)
    for hh in range(MLA_HEADS):
        sl = slice(hh * LANE, (hh + 1) * LANE)
        qmla_ref[:, sl] = (_rope_slab(qf[:, sl], rc, rs1, rs2) * MLA_SCALE).astype(qmla_ref.dtype)

    kv = _mm(ckv, wukv_ref[...], exact)
    kn = _segnorm(kv[:, :768], g3_ref, gt3_ref, invn3_ref, gain3_ref, None)
    for hh in range(MLA_HEADS):
        sl = slice(hh * LANE, (hh + 1) * LANE)
        kmla_ref[:, sl] = (kn[:, sl] + kr).astype(kmla_ref.dtype)
    vmla_ref[...] = kv[:, 768:].astype(vmla_ref.dtype)

    pooled = jnp.dot(pool_ref[...], nsa_row[:, 0:128], precision=HIGHEST,
                     preferred_element_type=F32) + pem_ref[...]
    cm = jnp.dot(pooled, wc_ref[...], precision=HIGHEST, preferred_element_type=F32)
    lane_c = lax.broadcasted_iota(jnp.int32, cm.shape, 1)
    is_k = lane_c < HEAD_DIM
    ms = jnp.sum(jnp.where(is_k, cm * cm, 0.0), axis=-1, keepdims=True) * (1.0 / HEAD_DIM)
    cmp_ref[...] = jnp.where(is_k, cm * lax.rsqrt(ms + RMS_EPS) * cgain_ref[...], cm)


def _const_spec(shape):
    return pl.BlockSpec(shape, lambda i: (0,) * len(shape))


def _project(x, pos, lw, *, tm, exact, qdtype):
    t = x.shape[0]
    assert t % tm == 0 and tm % CMP_BLOCK == 0
    nt = t // tm
    rc, rs1, rs2 = _rope_tables(pos)
    wdt = F32 if exact else BF16
    pool = np.zeros((tm // CMP_BLOCK, tm), np.float32)
    for r in range(tm // CMP_BLOCK):
        pool[r, r * CMP_BLOCK:(r + 1) * CMP_BLOCK] = 1.0 / CMP_BLOCK
    ins = [x, lw['attn_norm'], lw['w1'].astype(wdt), _G1, _GT1, _INVN1, lw['gain1'], lw['pass1'],
           rc, rs1, rs2, lw['wuq'].astype(wdt), _G2, _GT2, _INVN2, lw['gain2'],
           lw['wukv'].astype(wdt), _G3, _GT3, _INVN3, lw['gain3'],
           pool, lw['pe_mean'], lw['wc'], lw['cgain']]
    tok = lambda w: pl.BlockSpec((tm, w), lambda i: (i, 0))
    in_specs = [tok(D_MODEL)] + [_const_spec(tuple(np.shape(a))) for a in ins[1:8]] \
        + [tok(LANE)] * 3 + [_const_spec(tuple(np.shape(a))) for a in ins[11:]]
    widths = dict(qnsa=(768, qdtype), gates=(128, F32), nsarow=(256, F32), winrow=(128, F32),
                  qmla=(768, qdtype), ckv=(128, F32), kr=(128, F32), qd=(1024, qdtype),
                  diffrow=(256, F32), kmla=(768, qdtype), vmla=(384, qdtype),
                  nsarowb=(256, BF16), winrowb=(128, BF16), diffrowb=(256, BF16))
    out_shape = [jax.ShapeDtypeStruct((t, w), dt) for (w, dt) in widths.values()]
    out_specs = [tok(w) for (w, _) in widths.values()]
    out_shape.append(jax.ShapeDtypeStruct((t // CMP_BLOCK, LANE), F32))
    out_specs.append(pl.BlockSpec((tm // CMP_BLOCK, LANE), lambda i: (i, 0)))
    outs = pl.pallas_call(
        functools.partial(_proj_kernel, exact=exact),
        grid=(nt,), in_specs=in_specs, out_specs=out_specs, out_shape=out_shape,
        compiler_params=pltpu.CompilerParams(dimension_semantics=("arbitrary",),
                                             vmem_limit_bytes=VMEM_LIMIT),
        name="token_projection",
    )(*ins)
    res = dict(zip(list(widths.keys()) + ['cmp'], outs))
    return res


def _layer_weights(l, p):
    flat = _flat_gains(p['nsa_qk_gain'][l], p['mla_q_norm'][l], p['mla_kv_norm'][l],
                       p['diff_qk_gain'][l], p['mla_rope_gain'][l], p['mla_nope_gain'][l])
    lw = {}
    lw['attn_norm'] = p['attn_norm'][l].reshape(1, D_MODEL)
    lw['w1'] = _take_cols(p['w_in'][l], _S1[0])
    lw['gain1'] = jnp.take(flat, jnp.asarray(_S1[2])).reshape(1, W_Y)
    lw['pass1'] = jnp.asarray(_S1[3]).reshape(1, W_Y)
    wuq = _take_cols(p['mla_w_uq'][l], _S2[0])
    lw['wuq'] = jnp.concatenate([wuq, jnp.zeros((256 - MLA_Q_LORA, wuq.shape[1]), F32)], axis=0)
    lw['gain2'] = jnp.take(flat, jnp.asarray(_S2[2])).reshape(1, -1)
    lw['wukv'] = _take_cols(p['mla_w_ukv'][l], _S3[0])
    lw['gain3'] = jnp.take(flat, jnp.asarray(_S3[2])).reshape(1, -1)
    pe = p['nsa_cmp_pe'][l]
    lw['pe_mean'] = jnp.concatenate([jnp.mean(pe[0], axis=0), jnp.mean(pe[1], axis=0)]).reshape(1, LANE)
    wk, wv = p['nsa_cmp_w'][l][0], p['nsa_cmp_w'][l][1]
    z = jnp.zeros((HEAD_DIM, HEAD_DIM), F32)
    lw['wc'] = jnp.concatenate([jnp.concatenate([wk, z], axis=1), jnp.concatenate([z, wv], axis=1)], axis=0)
    lw['cgain'] = jnp.concatenate([p['nsa_qk_gain'][l][1], jnp.ones((HEAD_DIM,), F32)]).reshape(1, LANE)
    return lw


def _bucket_np(n):
    n = np.asarray(n)
    nf = np.maximum(n, 1).astype(np.float64)
    large = 16 + (np.log(nf / 16.0) / math.log(REL_MAX_DIST / 16.0) * 16.0).astype(np.int64)
    return np.where(n < 16, n, np.minimum(large, REL_BUCKETS - 1)).astype(np.int32)


_BUCKET_LB = [int(np.nonzero(_bucket_np(np.arange(4096)) >= b)[0][0]) for b in range(REL_BUCKETS)]
FAR_DIST = _BUCKET_LB[-1]
NEG = -1e30

TQ = 256
TK = 512
BAND_ROWS = 128
BAND_C = 2432
BAND_W = BAND_C + TK
assert BAND_C - (TK - 1) >= FAR_DIST and BAND_C % LANE == 0 and TQ % BAND_ROWS == 0

_BAND_BID = _bucket_np(np.maximum(
    BAND_C + np.arange(BAND_ROWS)[:, None] - np.arange(BAND_W)[None, :], 0))


def _bias_from_dist(dist, tab_ref, col):
    b = jnp.full(dist.shape, tab_ref[0 * _TAB_COLS + col], F32)
    for k in range(1, REL_BUCKETS):
        b = jnp.where(dist >= _BUCKET_LB[k], tab_ref[k * _TAB_COLS + col], b)
    return b


_TAB_COLS = NSA_HEADS + DIFF_HEADS


def _build_band(band_ref, bid_ref, tab_ref, col0, nheads):
    bid = bid_ref[...]
    for h in range(nheads):
        b = jnp.full(bid.shape, tab_ref[(REL_BUCKETS - 1) * _TAB_COLS + col0 + h], F32)
        for k in range(REL_BUCKETS - 2, -1, -1):
            b = jnp.where(bid == k, tab_ref[k * _TAB_COLS + col0 + h], b)
        band_ref[h] = b


def _band_tile(band_ref, h, delta):
    parts = []
    for rb in range(TQ // BAND_ROWS):
        c0 = BAND_C - jnp.minimum(delta + rb * BAND_ROWS, BAND_C)
        parts.append(band_ref[h, :, pl.ds(pl.multiple_of(c0, LANE), TK)])
    return jnp.concatenate(parts, axis=0)


def _causal_pairs(s):
    qi, ki = [], []
    for a in range(s // TQ):
        for b in range((a * TQ + TQ - 1) // TK + 1):
            qi.append(a)
            ki.append(b)
    return np.asarray(qi, np.int32), np.asarray(ki, np.int32)


def _softmax_step(s, m_ref, l_ref, acc_ref, idx, v, keep=None):
    m_old = m_ref[idx]
    m_new = jnp.maximum(m_old, jnp.max(s, axis=-1, keepdims=True))
    alpha = jnp.exp(m_old - m_new)
    p = jnp.exp(s - m_new)
    if keep is not None:
        p = jnp.where(keep, p, 0.0)
    l_ref[idx] = alpha * l_ref[idx] + jnp.sum(p, axis=-1, keepdims=True)
    m_ref[idx] = m_new
    acc_ref[idx] = alpha * acc_ref[idx] + jnp.dot(p.astype(BF16), v, preferred_element_type=F32)


def _init_softmax(m_ref, l_ref, acc_ref):
    m_ref[...] = jnp.full(m_ref.shape, NEG, F32)
    l_ref[...] = jnp.zeros(l_ref.shape, F32)
    acc_ref[...] = jnp.zeros(acc_ref.shape, F32)


def _tile_positions(q0, k0):
    rows = q0 + lax.broadcasted_iota(jnp.int32, (TQ, TK), 0)
    cols = k0 + lax.broadcasted_iota(jnp.int32, (TQ, TK), 1)
    return rows, cols


def _mla_attn_kernel(qt_ref, kt_ref, q_ref, k_ref, v_ref, o_ref, m_sc, l_sc, acc_sc):
    step = pl.program_id(0)
    qi, ki = qt_ref[step], kt_ref[step]
    q0, k0 = qi * TQ, ki * TK

    @pl.when(ki == 0)
    def _():
        _init_softmax(m_sc, l_sc, acc_sc)

    def update(masked):
        if masked:
            rows, cols = _tile_positions(q0, k0)
            keep = cols <= rows
        for h in range(MLA_HEADS):
            sl = slice(h * LANE, (h + 1) * LANE)
            s = _mm_nt(q_ref[:, sl], k_ref[:, sl], False)
            if masked:
                s = jnp.where(keep, s, NEG)
            pr = slice((h // 2) * LANE, (h // 2 + 1) * LANE)
            _softmax_step(s, m_sc, l_sc, acc_sc, h, v_ref[:, pr])

    on_diag = k0 + TK - 1 > q0
    pl.when(on_diag)(lambda: update(True))
    pl.when(jnp.logical_not(on_diag))(lambda: update(False))

    @pl.when(ki == (q0 + TQ - 1) // TK)
    def _():
        lane = lax.broadcasted_iota(jnp.int32, (TQ, LANE), 1)
        for pi in range(MLA_HEADS // 2):
            lo = acc_sc[2 * pi] / l_sc[2 * pi]
            hi = acc_sc[2 * pi + 1] / l_sc[2 * pi + 1]
            o_ref[:, pi * LANE:(pi + 1) * LANE] = jnp.where(lane < MLA_V, lo, hi)


def _mla_attention(q, k, v):
    s = q.shape[0]
    qt, kt = _causal_pairs(s)
    grid_spec = pltpu.PrefetchScalarGridSpec(
        num_scalar_prefetch=2, grid=(qt.shape[0],),
        in_specs=[pl.BlockSpec((TQ, 768), lambda i, a, b: (a[i], 0)),
                  pl.BlockSpec((TK, 768), lambda i, a, b: (b[i], 0)),
                  pl.BlockSpec((TK, 384), lambda i, a, b: (b[i], 0))],
        out_specs=pl.BlockSpec((TQ, 384), lambda i, a, b: (a[i], 0)),
        scratch_shapes=[pltpu.VMEM((MLA_HEADS, TQ, 1), F32), pltpu.VMEM((MLA_HEADS, TQ, 1), F32),
                        pltpu.VMEM((MLA_HEADS, TQ, LANE), F32)])
    return pl.pallas_call(
        _mla_attn_kernel, grid_spec=grid_spec, out_shape=jax.ShapeDtypeStruct((s, 384), F32),
        compiler_params=pltpu.CompilerParams(dimension_semantics=("arbitrary",),
                                             vmem_limit_bytes=VMEM_LIMIT),
        name="mla_prompt_attention")(qt, kt, q, k, v)


N_DMAPS = 2 * DIFF_HEADS


def _diff_attn_kernel(qt_ref, kt_ref, tab_ref, bid_ref, q_ref, kv_ref, par_ref, o_ref,
                      band_sc, m_sc, l_sc, acc_sc):
    step = pl.program_id(0)
    qi, ki = qt_ref[step], kt_ref[step]
    q0, k0 = qi * TQ, ki * TK

    @pl.when(step == 0)
    def _():
        _build_band(band_sc, bid_ref, tab_ref, NSA_HEADS, DIFF_HEADS)

    @pl.when(ki == 0)
    def _():
        _init_softmax(m_sc, l_sc, acc_sc)

    def update(masked):
        if masked:
            rows, cols = _tile_positions(q0, k0)
            keep = cols <= rows
        k = kv_ref[:, 0:LANE]
        v = kv_ref[:, LANE:2 * LANE]
        for h in range(DIFF_HEADS):
            bias = _band_tile(band_sc, h, q0 - k0)
            for m in range(2):
                r = 2 * h + m
                s = _mm_nt(q_ref[:, r * LANE:(r + 1) * LANE], k, False) + bias
                if masked:
                    s = jnp.where(keep, s, NEG)
                _softmax_step(s, m_sc, l_sc, acc_sc, r, v)

    on_diag = k0 + TK - 1 > q0
    pl.when(on_diag)(lambda: update(True))
    pl.when(jnp.logical_not(on_diag))(lambda: update(False))

    @pl.when(ki == (q0 + TQ - 1) // TK)
    def _():
        lane = lax.broadcasted_iota(jnp.int32, (TQ, LANE), 1)
        gain = par_ref[0:1, :]
        lam = par_ref[1:2, :]
        for h in range(DIFF_HEADS):
            g = h // 2
            o = acc_sc[2 * h] / l_sc[2 * h] - lam * (acc_sc[2 * h + 1] / l_sc[2 * h + 1])
            o = jnp.where((lane >= g * DIFF_V) & (lane < (g + 1) * DIFF_V), o, 0.0)
            ms = jnp.sum(o * o, axis=-1, keepdims=True) * (1.0 / DIFF_V)
            o_ref[:, h * LANE:(h + 1) * LANE] = o * lax.rsqrt(ms + RMS_EPS) * gain


def _diff_params(l, p):
    lam_init = 0.8 - 0.6 * math.exp(-0.3 * l)
    lv = p['diff_lambda'][l].astype(F32)
    lam = jnp.exp(jnp.sum(lv[0] * lv[1])) - jnp.exp(jnp.sum(lv[2] * lv[3])) + lam_init
    gain = jnp.tile(p['diff_head_gain'][l].astype(F32), 2) * (1.0 - lam_init)
    return jnp.concatenate([gain.reshape(1, LANE), jnp.full((1, LANE), lam, F32),
                            jnp.zeros((6, LANE), F32)], axis=0)


def _diff_attention(q, kv, table, dpar):
    s = q.shape[0]
    qt, kt = _causal_pairs(s)
    smem = pl.BlockSpec(memory_space=pltpu.SMEM)
    grid_spec = pltpu.PrefetchScalarGridSpec(
        num_scalar_prefetch=2, grid=(qt.shape[0],),
        in_specs=[smem,
                  pl.BlockSpec((BAND_ROWS, BAND_W), lambda i, a, b: (0, 0)),
                  pl.BlockSpec((TQ, 1024), lambda i, a, b: (a[i], 0)),
                  pl.BlockSpec((TK, 256), lambda i, a, b: (b[i], 0)),
                  pl.BlockSpec((8, LANE), lambda i, a, b: (0, 0))],
        out_specs=pl.BlockSpec((TQ, 512), lambda i, a, b: (a[i], 0)),
        scratch_shapes=[pltpu.VMEM((DIFF_HEADS, BAND_ROWS, BAND_W), F32),
                        pltpu.VMEM((N_DMAPS, TQ, 1), F32), pltpu.VMEM((N_DMAPS, TQ, 1), F32),
                        pltpu.VMEM((N_DMAPS, TQ, LANE), F32)])
    return pl.pallas_call(
        _diff_attn_kernel, grid_spec=grid_spec, out_shape=jax.ShapeDtypeStruct((s, 512), F32),
        compiler_params=pltpu.CompilerParams(dimension_semantics=("arbitrary",),
                                             vmem_limit_bytes=VMEM_LIMIT),
        name="diff_prompt_attention")(qt, kt, table.reshape(-1), _BAND_BID, q, kv, dpar)


SEL_FORCED = 1e30
SEL_INVALID = -1e30
SEL_TAKEN = -2e30


def _select_blocks(score, blk):
    sel = jnp.zeros(score.shape, jnp.bool_)
    for _ in range(TOP_N):
        m = jnp.max(score, axis=-1, keepdims=True)
        idx = jnp.min(jnp.where(score == m, blk, 1 << 20), axis=-1, keepdims=True)
        hit = blk == idx
        sel = sel | hit
        score = jnp.where(hit, SEL_TAKEN, score)
    return sel


def _masked_softmax_rows(s, keep):
    s = jnp.where(keep, s, NEG)
    m = jnp.max(s, axis=-1, keepdims=True)
    e = jnp.where(keep, jnp.exp(s - m), 0.0)
    z = jnp.sum(e, axis=-1, keepdims=True)
    return e / jnp.where(z > 0.0, z, 1.0)


def _nsa_attn_kernel(qt_ref, kt_ref, tab_ref, bid_ref, q_ref, gates_ref, ksel_ref, kwin_ref, cmp_ref,
                     pair_ref, o_ref, band_sc, sel_sc, oc_sc, m_sc, l_sc, acc_sc, mw_sc, lw_sc, accw_sc):
    step = pl.program_id(0)
    qi, ki = qt_ref[step], kt_ref[step]
    q0, k0 = qi * TQ, ki * TK
    nc = cmp_ref.shape[0]
    ns = pair_ref.shape[1]

    @pl.when(step == 0)
    def _():
        _build_band(band_sc, bid_ref, tab_ref, 0, NSA_HEADS)

    @pl.when(ki == 0)
    def _():
        _init_softmax(m_sc, l_sc, acc_sc)
        _init_softmax(mw_sc, lw_sc, accw_sc)
        cm = cmp_ref[...].astype(BF16)
        qpos = q0 + lax.broadcasted_iota(jnp.int32, (TQ, nc), 0)
        cend = (lax.broadcasted_iota(jnp.int32, (TQ, nc), 1) + 1) * CMP_BLOCK - 1
        dist = qpos - cend
        keep = dist >= 0
        imp = jnp.zeros((TQ, nc), F32)
        for h in range(NSA_HEADS):
            s = _mm_nt(q_ref[:, h * LANE:(h + 1) * LANE], cm, False)
            p = _masked_softmax_rows(s + _bias_from_dist(dist, tab_ref, h), keep)
            oc_sc[h] = jnp.dot(p.astype(BF16), cm, preferred_element_type=F32)
            imp = imp + p
        imp = jnp.dot(imp, pair_ref[...], precision=HIGHEST, preferred_element_type=F32)
        blk = lax.broadcasted_iota(jnp.int32, (TQ, ns), 1)
        qp = q0 + lax.broadcasted_iota(jnp.int32, (TQ, ns), 0)
        cur = qp // SEL_BLOCK
        forced = (blk == 0) | (blk == cur) | (blk == cur - 1)
        valid = blk * SEL_BLOCK <= qp
        score = jnp.where(valid, jnp.where(forced, SEL_FORCED, imp), SEL_INVALID)
        sel_sc[...] = _select_blocks(score, blk).astype(F32)

    rows, cols = _tile_positions(q0, k0)
    dist = rows - cols
    eb = lax.broadcasted_iota(jnp.int32, (ns, TK), 0)
    ek = k0 + lax.broadcasted_iota(jnp.int32, (ns, TK), 1)
    expand = jnp.where(eb == ek // SEL_BLOCK, 1.0, 0.0).astype(BF16)
    chosen = jnp.dot(sel_sc[...].astype(BF16), expand, preferred_element_type=F32)
    keep_s = (chosen > 0.5) & (dist >= 0)
    in_win = k0 + TK - 1 >= q0 - (WINDOW - 1)
    kv = ksel_ref[...]
    for h in range(NSA_HEADS):
        s = _mm_nt(q_ref[:, h * LANE:(h + 1) * LANE], kv, False) + _band_tile(band_sc, h, q0 - k0)
        _softmax_step(jnp.where(keep_s, s, NEG), m_sc, l_sc, acc_sc, h, kv)

    @pl.when(in_win)
    def _():
        keep_w = (dist >= 0) & (dist < WINDOW)
        kw = kwin_ref[...]
        for h in range(NSA_HEADS):
            s = _mm_nt(q_ref[:, h * LANE:(h + 1) * LANE], kw, False) + _band_tile(band_sc, h, q0 - k0)
            _softmax_step(jnp.where(keep_w, s, NEG), mw_sc, lw_sc, accw_sc, h, kw, keep=keep_w)

    @pl.when(ki == (q0 + TQ - 1) // TK)
    def _():
        lane = lax.broadcasted_iota(jnp.int32, (TQ, LANE), 1)
        for h in range(NSA_HEADS):
            o = (gates_ref[:, 3 * h:3 * h + 1] * oc_sc[h]
                 + gates_ref[:, 3 * h + 1:3 * h + 2] * (acc_sc[h] / l_sc[h])
                 + gates_ref[:, 3 * h + 2:3 * h + 3] * (accw_sc[h] / lw_sc[h]))
            o_ref[:, h * LANE:(h + 1) * LANE] = jnp.where(lane >= HEAD_DIM, o, 0.0)


def _first_window_tile(q0):
    return jnp.maximum(q0 - (WINDOW - 1), 0) // TK


def _nsa_attention(q, gates, nsarow_b, winrow_b, cmp, table):
    s = q.shape[0]
    nc, ns = s // CMP_BLOCK, s // SEL_BLOCK
    qt, kt = _causal_pairs(s)
    pair = np.zeros((nc, ns), np.float32)
    pair[np.arange(nc), np.arange(nc) // (SEL_BLOCK // CMP_BLOCK)] = 1.0
    smem = pl.BlockSpec(memory_space=pltpu.SMEM)
    hq = (NSA_HEADS, TQ)
    grid_spec = pltpu.PrefetchScalarGridSpec(
        num_scalar_prefetch=2, grid=(qt.shape[0],),
        in_specs=[smem,
                  pl.BlockSpec((BAND_ROWS, BAND_W), lambda i, a, b: (0, 0)),
                  pl.BlockSpec((TQ, 768), lambda i, a, b: (a[i], 0)),
                  pl.BlockSpec((TQ, LANE), lambda i, a, b: (a[i], 0)),
                  pl.BlockSpec((TK, LANE), lambda i, a, b: (b[i], 1)),
                  pl.BlockSpec((TK, LANE), lambda i, a, b: (jnp.maximum(b[i], _first_window_tile(a[i] * TQ)), 0)),
                  pl.BlockSpec((nc, LANE), lambda i, a, b: (0, 0)),
                  pl.BlockSpec((nc, ns), lambda i, a, b: (0, 0))],
        out_specs=pl.BlockSpec((TQ, 768), lambda i, a, b: (a[i], 0)),
        scratch_shapes=[pltpu.VMEM((NSA_HEADS, BAND_ROWS, BAND_W), F32),
                        pltpu.VMEM((TQ, ns), F32), pltpu.VMEM(hq + (LANE,), F32),
                        pltpu.VMEM(hq + (1,), F32), pltpu.VMEM(hq + (1,), F32), pltpu.VMEM(hq + (LANE,), F32),
                        pltpu.VMEM(hq + (1,), F32), pltpu.VMEM(hq + (1,), F32), pltpu.VMEM(hq + (LANE,), F32)])
    return pl.pallas_call(
        _nsa_attn_kernel, grid_spec=grid_spec, out_shape=jax.ShapeDtypeStruct((s, 768), F32),
        compiler_params=pltpu.CompilerParams(dimension_semantics=("arbitrary",),
                                             vmem_limit_bytes=VMEM_LIMIT),
        name="nsa_prompt_attention")(qt, kt, table.reshape(-1), _BAND_BID, q, gates, nsarow_b, winrow_b,
                                     cmp, pair)


ROUTE_E0 = N_GROUPS


def _ffn_kernel(x_ref, nsa_ref, mla_ref, dif_ref, won_ref, wom_ref, wod_ref, fn_ref, wr_ref, br_ref,
                w1_ref, w3_ref, w2_ref, y_ref, xn_sc, hb_sc, comb_sc, acc_sc, *, exact):
    e = pl.program_id(1)
    tm = x_ref.shape[0]
    lane = lax.broadcasted_iota(jnp.int32, (tm, LANE), 1)

    @pl.when(e == 0)
    def _():
        attn = (_mm(nsa_ref[...], won_ref[...], exact) + _mm(mla_ref[...], wom_ref[...], exact)
                + _mm(dif_ref[...], wod_ref[...], exact))
        xn = x_ref[...] + attn
        xn_sc[...] = xn
        h = xn * lax.rsqrt(jnp.mean(xn * xn, axis=-1, keepdims=True) + RMS_EPS) * fn_ref[...]
        hb_sc[...] = h.astype(BF16)
        logits = jnp.dot(h, wr_ref[...], precision=HIGHEST, preferred_element_type=F32) + br_ref[...]
        big = 1 << 20
        is_g = lane < N_GROUPS
        gl = jnp.where(is_g, logits, NEG)
        gmax = jnp.max(gl, axis=-1, keepdims=True)
        gidx = jnp.min(jnp.where(gl == gmax, lane, big), axis=-1, keepdims=True)
        p_g = 1.0 / jnp.sum(jnp.where(is_g, jnp.exp(gl - gmax), 0.0), axis=-1, keepdims=True)
        in_grp = (lane >= ROUTE_E0) & (lane < ROUTE_E0 + N_EXPERTS) \
            & ((lane - ROUTE_E0) // EXPERTS_PER_GROUP == gidx)
        e1 = jnp.where(in_grp, logits, NEG)
        v1 = jnp.max(e1, axis=-1, keepdims=True)
        i1 = jnp.min(jnp.where(e1 == v1, lane, big), axis=-1, keepdims=True)
        e2 = jnp.where(lane == i1, NEG, e1)
        v2 = jnp.max(e2, axis=-1, keepdims=True)
        i2 = jnp.min(jnp.where(e2 == v2, lane, big), axis=-1, keepdims=True)
        t = jnp.exp(v2 - v1)
        w_top = p_g / (1.0 + t)
        comb_sc[...] = jnp.where(lane == i1, w_top, 0.0) + jnp.where(lane == i2, w_top * t, 0.0)
        acc_sc[...] = jnp.zeros(acc_sc.shape, F32)

    hb = hb_sc[...]
    a = jnp.dot(hb, w1_ref[0], preferred_element_type=F32)
    b = jnp.dot(hb, w3_ref[0], preferred_element_type=F32)
    cw = jnp.sum(jnp.where(lane == ROUTE_E0 + e, comb_sc[...], 0.0), axis=-1, keepdims=True)
    act = a * jax.nn.sigmoid(a) * b * cw
    acc_sc[...] += jnp.dot(act.astype(BF16), w2_ref[0], preferred_element_type=F32)

    @pl.when(e == N_EXPERTS - 1)
    def _():
        y_ref[...] = xn_sc[...] + acc_sc[...]


def _ffn_weights(l, p):
    wo = p['w_out'][l]
    z = wo.shape[1]
    rows_n = np.full((NSA_HEADS * LANE,), wo.shape[0], np.int32)
    for h in range(NSA_HEADS):
        rows_n[h * LANE + HEAD_DIM + np.arange(HEAD_DIM)] = h * HEAD_DIM + np.arange(HEAD_DIM)
    rows_d = np.full((DIFF_HEADS * LANE,), wo.shape[0], np.int32)
    for h in range(DIFF_HEADS):
        rows_d[h * LANE + (h // 2) * DIFF_V + np.arange(DIFF_V)] = 768 + h * DIFF_V + np.arange(DIFF_V)
    woz = jnp.concatenate([wo, jnp.zeros((1, z), wo.dtype)], axis=0)
    fw = {}
    fw['won'] = jnp.take(woz, jnp.asarray(rows_n), axis=0)
    fw['wom'] = wo[384:768]
    fw['wod'] = jnp.take(woz, jnp.asarray(rows_d), axis=0)
    fw['fnorm'] = p['ffn_norm'][l].reshape(1, D_MODEL)
    wrt = jnp.transpose(p['moe_w_router'][l], (1, 0, 2)).reshape(D_MODEL, N_EXPERTS)
    pad = jnp.zeros((D_MODEL, LANE - N_GROUPS - N_EXPERTS), F32)
    fw['wr'] = jnp.concatenate([p['moe_w_group'][l], wrt, pad], axis=1)
    fw['br'] = jnp.concatenate([p['moe_b_group'][l], p['moe_b_router'][l].reshape(-1),
                                jnp.zeros((LANE - N_GROUPS - N_EXPERTS,), F32)]).reshape(1, LANE)
    fw['w1'] = p['moe_w1'][l].astype(BF16)
    fw['w3'] = p['moe_w3'][l].astype(BF16)
    fw['w2'] = p['moe_w2'][l].astype(BF16)
    return fw


def _ffn(x, nsa, mla, dif, fw, *, tm, exact):
    t = x.shape[0]
    assert t % tm == 0
    wdt = F32 if exact else BF16
    tok = lambda w: pl.BlockSpec((tm, w), lambda i, e: (i, 0))
    cst = lambda a: pl.BlockSpec(tuple(a.shape), lambda i, e: (0,) * a.ndim)
    won, wom, wod = fw['won'].astype(wdt), fw['wom'].astype(wdt), fw['wod'].astype(wdt)
    in_specs = [tok(D_MODEL), tok(768), tok(384), tok(512), cst(won), cst(wom), cst(wod),
                cst(fw['fnorm']), cst(fw['wr']), cst(fw['br']),
                pl.BlockSpec((1, D_MODEL, D_EXPERT), lambda i, e: (e, 0, 0)),
                pl.BlockSpec((1, D_MODEL, D_EXPERT), lambda i, e: (e, 0, 0)),
                pl.BlockSpec((1, D_EXPERT, D_MODEL), lambda i, e: (e, 0, 0))]
    return pl.pallas_call(
        functools.partial(_ffn_kernel, exact=exact),
        grid=(t // tm, N_EXPERTS), in_specs=in_specs, out_specs=tok(D_MODEL),
        out_shape=jax.ShapeDtypeStruct((t, D_MODEL), F32),
        scratch_shapes=[pltpu.VMEM((tm, D_MODEL), F32), pltpu.VMEM((tm, D_MODEL), BF16),
                        pltpu.VMEM((tm, LANE), F32), pltpu.VMEM((tm, D_MODEL), F32)],
        compiler_params=pltpu.CompilerParams(dimension_semantics=("arbitrary", "arbitrary"),
                                             vmem_limit_bytes=VMEM_LIMIT),
        name="outproj_moe")(x, nsa, mla, dif, won, wom, wod, fw['fnorm'], fw['wr'], fw['br'],
                            fw['w1'], fw['w3'], fw['w2'])


PAGES_PER_STEP = 8
CHUNK = PAGES_PER_STEP * PAGE_SIZE
TAIL = 8
ROWS8 = 8
N_SEL_PAD = 384


def _bias_rows(dist_row, tab_ref, col0, nheads, rows_per_head):
    w = dist_row.shape[1]
    masks = [dist_row >= _BUCKET_LB[k] for k in range(1, REL_BUCKETS)]
    row = lax.broadcasted_iota(jnp.int32, (ROWS8, w), 0)
    out = jnp.zeros((ROWS8, w), F32)
    for h in range(nheads):
        b = jnp.full(dist_row.shape, tab_ref[col0 + h], F32)
        for k in range(1, REL_BUCKETS):
            b = jnp.where(masks[k - 1], tab_ref[k * _TAB_COLS + col0 + h], b)
        out = jnp.where(row // rows_per_head == h, b, out)
    return out


def _far_bias_rows(tab_ref, col0, nheads, rows_per_head, w):
    row = lax.broadcasted_iota(jnp.int32, (ROWS8, w), 0)
    out = jnp.zeros((ROWS8, w), F32)
    for h in range(nheads):
        out = jnp.where(row // rows_per_head == h, tab_ref[(REL_BUCKETS - 1) * _TAB_COLS + col0 + h], out)
    return out


def _split_bf16(x):
    hi = x.astype(BF16)
    return hi, (x - hi.astype(F32)).astype(BF16)


def _nt_bf16x3(q, k):
    qh, ql = _split_bf16(q)
    kh, kl = _split_bf16(k)
    dn = (((1,), (1,)), ((), ()))
    a = lax.dot_general(jnp.concatenate([qh, ql], axis=0), kh, dn, preferred_element_type=F32)
    b = lax.dot_general(qh, kl, dn, preferred_element_type=F32)
    return a[:ROWS8] + a[ROWS8:] + b


def _rows_softmax_step(s, keep, m_ref, l_ref, acc_ref, v):
    if keep is not None:
        s = jnp.where(keep, s, NEG)
    m_old = m_ref[...]
    m_new = jnp.maximum(m_old, jnp.max(s, axis=-1, keepdims=True))
    alpha = jnp.exp(m_old - m_new)
    p = jnp.exp(s - m_new)
    if keep is not None:
        p = jnp.where(keep, p, 0.0)
    l_ref[...] = alpha * l_ref[...] + jnp.sum(p, axis=-1, keepdims=True)
    m_ref[...] = m_new
    acc_ref[...] = alpha * acc_ref[...] + jnp.dot(p.astype(BF16), v, preferred_element_type=F32)


def _sample_kernel(*refs):
    np_ = PAGES_PER_STEP
    (pt_ref, tab_ref) = refs[0:2]
    (qn8_ref, qg8_ref, qr8_ref, qd8_ref, gates8_ref, tnsa_ref, tmla_ref, tdiff_ref, twin_ref,
     winbuf_ref) = refs[2:12]
    nsa_pages = refs[12:12 + np_]
    mla_pages = refs[12 + np_:12 + 2 * np_]
    diff_pages = refs[12 + 2 * np_:12 + 3 * np_]
    (wukt_ref, ngain_ref, wuv_ref, pem_ref, wc_ref, cgain_ref, pair_ref, dpar_ref) = refs[12 + 3 * np_:20 + 3 * np_]
    (onsa_ref, omla_ref, odiff_ref) = refs[20 + 3 * np_:23 + 3 * np_]
    (ut_sc, mm_sc, lm_sc, accm_sc, md_sc, ld_sc, accd_sc, pooled_sc, ksvs_sc, selbuf_sc) = refs[23 + 3 * np_:]
    del pt_ref
    c = pl.program_id(1)
    nchunk = pl.num_programs(1)
    qpos = PAST_LEN

    @pl.when(c == 0)
    def _():
        for m_ref, l_ref, a_ref in ((mm_sc, lm_sc, accm_sc), (md_sc, ld_sc, accd_sc)):
            m_ref[...] = jnp.full(m_ref.shape, NEG, F32)
            l_ref[...] = jnp.zeros(l_ref.shape, F32)
            a_ref[...] = jnp.zeros(a_ref.shape, F32)
        ut_sc[...] = jnp.dot(qg8_ref[0] * ngain_ref[...], wukt_ref[0:MLA_HEADS * MLA_NOPE, :],
                             precision=HIGHEST, preferred_element_type=F32)
        ksvs_sc[pl.ds(PAST_LEN, SEL_BLOCK), :] = jnp.zeros((SEL_BLOCK, LANE), F32)

    def mla_update(x, keep):
        r = x.shape[0]
        c_lat = x[:, 0:MLA_KV_LORA].astype(BF16)
        kkt = _mm_nt(wukt_ref[...], c_lat, False)
        ssq = jnp.sum((kkt * kkt).reshape(ROWS8, MLA_NOPE, r), axis=1)
        num = _mm_nt(ut_sc[...], c_lat, False)
        rope = _mm_nt(qr8_ref[0], x[:, MLA_ROPE:MLA_ROPE + LANE], False)
        s = num * lax.rsqrt(ssq * (1.0 / MLA_NOPE) + RMS_EPS) + rope
        _rows_softmax_step(s, keep, mm_sc, lm_sc, accm_sc, c_lat)

    def diff_update(x, bias, keep):
        s = _nt_bf16x3(qd8_ref[0], x[:, 0:LANE]) + bias
        _rows_softmax_step(s, keep, md_sc, ld_sc, accd_sc, x[:, LANE:2 * LANE].astype(BF16))

    xn = jnp.concatenate([r[0, 0] for r in nsa_pages], axis=0)
    xm = jnp.concatenate([r[0, 0] for r in mla_pages], axis=0)
    xd = jnp.concatenate([r[0, 0] for r in diff_pages], axis=0)
    k0 = c * CHUNK
    mla_update(xm, None)
    near = k0 + CHUNK - 1 > qpos - FAR_DIST
    dist_row = qpos - (k0 + lax.broadcasted_iota(jnp.int32, (1, CHUNK), 1))

    @pl.when(near)
    def _():
        diff_update(xd, _bias_rows(dist_row, tab_ref, NSA_HEADS, DIFF_HEADS, 2), None)

    @pl.when(jnp.logical_not(near))
    def _():
        diff_update(xd, _far_bias_rows(tab_ref, NSA_HEADS, DIFF_HEADS, 2, CHUNK), None)

    nblk = CHUNK // CMP_BLOCK
    pooled = jnp.sum(xn[:, 0:LANE].reshape(nblk, CMP_BLOCK, LANE), axis=1) * (1.0 / CMP_BLOCK)
    pooled_sc[pl.ds(pl.multiple_of(c * nblk, nblk), nblk), :] = pooled
    ksvs_sc[pl.ds(pl.multiple_of(k0, CHUNK), CHUNK), :] = xn[:, LANE:2 * LANE]

    @pl.when(c == nchunk - 1)
    def _():
        tail_keep = lax.broadcasted_iota(jnp.int32, (ROWS8, TAIL), 1) == 0
        mla_update(tmla_ref[0], tail_keep)
        diff_update(tdiff_ref[0], _bias_rows(jnp.zeros((1, TAIL), jnp.int32), tab_ref, NSA_HEADS, DIFF_HEADS, 2),
                    tail_keep)
        ksvs_sc[pl.ds(PAST_LEN, TAIL), :] = tnsa_ref[0][:, LANE:2 * LANE]

        row8 = lax.broadcasted_iota(jnp.int32, (ROWS8, LANE), 0)
        lane8 = lax.broadcasted_iota(jnp.int32, (ROWS8, LANE), 1)
        lat = accm_sc[...] / lm_sc[...]
        o = jnp.dot(lat, wuv_ref[...], precision=HIGHEST, preferred_element_type=F32)
        r384 = lax.broadcasted_iota(jnp.int32, o.shape, 0)
        l384 = lax.broadcasted_iota(jnp.int32, o.shape, 1)
        omla_ref[0] = jnp.sum(jnp.where(l384 // MLA_V == r384, o, 0.0), axis=0, keepdims=True)

        od = accd_sc[...] / ld_sc[...]
        a = od - dpar_ref[1:2, :] * pltpu.roll(od, ROWS8 - 1, 0)
        grp = row8 // 4
        a = jnp.where((lane8 >= grp * DIFF_V) & (lane8 < (grp + 1) * DIFF_V), a, 0.0)
        ms = jnp.sum(a * a, axis=-1, keepdims=True) * (1.0 / DIFF_V)
        odiff_ref[0] = a * lax.rsqrt(ms + RMS_EPS) * dpar_ref[0:1, :]

        qn8 = qn8_ref[0]
        nc = pooled_sc.shape[0]
        cm = jnp.dot(pooled_sc[...] + pem_ref[...], wc_ref[...], precision=HIGHEST, preferred_element_type=F32)
        lane_c = lax.broadcasted_iota(jnp.int32, cm.shape, 1)
        is_k = lane_c < HEAD_DIM
        msc = jnp.sum(jnp.where(is_k, cm * cm, 0.0), axis=-1, keepdims=True) * (1.0 / HEAD_DIM)
        cm = jnp.where(is_k, cm * lax.rsqrt(msc + RMS_EPS) * cgain_ref[...], cm)
        dist_c = qpos - ((lax.broadcasted_iota(jnp.int32, (1, nc), 1) + 1) * CMP_BLOCK - 1)
        s_c = _mm_nt(qn8, cm, True) + _bias_rows(dist_c, tab_ref, 0, NSA_HEADS, 1)
        p_c = _masked_softmax_rows(s_c, dist_c >= 0)
        o_c = jnp.dot(p_c.astype(BF16), cm.astype(BF16), preferred_element_type=F32)
        rowc = lax.broadcasted_iota(jnp.int32, p_c.shape, 0)
        imp = jnp.sum(jnp.where(rowc < NSA_HEADS, p_c, 0.0), axis=0, keepdims=True)
        imp = jnp.dot(imp, pair_ref[...], precision=HIGHEST, preferred_element_type=F32)

        blk = lax.broadcasted_iota(jnp.int32, (1, N_SEL_PAD), 1)
        cur = qpos // SEL_BLOCK
        forced = (blk == 0) | (blk == cur) | (blk == cur - 1)
        score = jnp.where(blk * SEL_BLOCK <= qpos, jnp.where(forced, SEL_FORCED, imp), SEL_INVALID)
        lane_s = lax.broadcasted_iota(jnp.int32, (1, TOP_N * SEL_BLOCK), 1)
        pos = jnp.zeros((1, TOP_N * SEL_BLOCK), jnp.int32)
        for t in range(TOP_N):
            top = jnp.max(score)
            idx = jnp.min(jnp.where(score == top, blk, 1 << 20))
            score = jnp.where(blk == idx, SEL_TAKEN, score)
            selbuf_sc[t * SEL_BLOCK:(t + 1) * SEL_BLOCK, :] = ksvs_sc[pl.ds(pl.multiple_of(idx * SEL_BLOCK, SEL_BLOCK), SEL_BLOCK), :]
            pos = jnp.where(lane_s // SEL_BLOCK == t, idx * SEL_BLOCK + lane_s % SEL_BLOCK, pos)
        sb = selbuf_sc[...]
        dist_s = qpos - pos
        s_s = _mm_nt(qn8, sb, True) + _bias_rows(dist_s, tab_ref, 0, NSA_HEADS, 1)
        p_s = _masked_softmax_rows(s_s, dist_s >= 0)
        o_s = jnp.dot(p_s.astype(BF16), sb.astype(BF16), preferred_element_type=F32)

        wb = winbuf_ref[0]
        nw = wb.shape[0]
        tw = twin_ref[0]
        dist_w = nw - lax.broadcasted_iota(jnp.int32, (1, nw), 1)
        keep_w = dist_w < WINDOW
        s_w = jnp.where(keep_w, _mm_nt(qn8, wb, True) + _bias_rows(dist_w, tab_ref, 0, NSA_HEADS, 1), NEG)
        s_n = jnp.where(tail_keep, _mm_nt(qn8, tw, True)
                        + _bias_rows(jnp.zeros((1, TAIL), jnp.int32), tab_ref, 0, NSA_HEADS, 1), NEG)
        m_w = jnp.maximum(jnp.max(s_w, axis=-1, keepdims=True), jnp.max(s_n, axis=-1, keepdims=True))
        e_w = jnp.where(keep_w, jnp.exp(s_w - m_w), 0.0)
        e_n = jnp.where(tail_keep, jnp.exp(s_n - m_w), 0.0)
        z_w = jnp.sum(e_w, axis=-1, keepdims=True) + jnp.sum(e_n, axis=-1, keepdims=True)
        o_w = (jnp.dot(e_w.astype(BF16), wb.astype(BF16), preferred_element_type=F32)
               + jnp.dot(e_n.astype(BF16), tw.astype(BF16), preferred_element_type=F32)) / z_w

        g = gates8_ref[0]
        onsa_ref[0] = g[:, 0:1] * o_c + g[:, 1:2] * o_s + g[:, 2:3] * o_w


def _sample_weights(l, p):
    wukv = p['mla_w_ukv'][l]
    cols_k = np.concatenate([h * (MLA_NOPE + MLA_V) + np.arange(MLA_NOPE) for h in range(MLA_HEADS)])
    cols_v = np.concatenate([h * (MLA_NOPE + MLA_V) + MLA_NOPE + np.arange(MLA_V) for h in range(MLA_HEADS)])
    wukt = jnp.transpose(jnp.take(wukv, jnp.asarray(cols_k), axis=1))
    sw = {}
    sw['wukt'] = jnp.concatenate([wukt, jnp.zeros((ROWS8 * MLA_NOPE - wukt.shape[0], MLA_KV_LORA), F32)], axis=0)
    sw['ngain'] = jnp.tile(p['mla_nope_gain'][l][1], MLA_HEADS).reshape(1, -1)
    sw['wuv'] = jnp.take(wukv, jnp.asarray(cols_v), axis=1)
    return sw


def _sample_mixers(l, page_table, cache_nsa, cache_mla, cache_diff, win_buf, pr, lw, sw, table, dpar):
    nb = page_table.shape[0]
    npages = page_table.shape[1]
    assert npages * PAGE_SIZE == PAST_LEN and npages % PAGES_PER_STEP == 0
    nchunk = npages // PAGES_PER_STEP
    pad_rows = lambda a, n: jnp.concatenate([a, jnp.zeros((a.shape[0], n - a.shape[1], a.shape[2]), a.dtype)], axis=1)
    qn8 = pad_rows(pr['qnsa'].reshape(nb, NSA_HEADS, LANE), ROWS8)
    qm = pr['qmla'].reshape(nb, MLA_HEADS, LANE)
    eye = jnp.eye(ROWS8, MLA_HEADS, dtype=F32)
    qg8 = (eye[None, :, :, None] * qm[:, None, :, :MLA_NOPE]).reshape(nb, ROWS8, MLA_HEADS * MLA_NOPE)
    qr8 = pad_rows(jnp.concatenate([jnp.zeros((nb, MLA_HEADS, LANE - MLA_ROPE), F32),
                                    qm[:, :, MLA_NOPE:MLA_NOPE + MLA_ROPE]], axis=2), ROWS8)
    qd8 = pr['qd'].reshape(nb, N_DMAPS, LANE)
    gates8 = pad_rows(jnp.concatenate([pr['gates'][:, :3 * NSA_HEADS].reshape(nb, NSA_HEADS, 3),
                                       jnp.zeros((nb, NSA_HEADS, LANE - 3), F32)], axis=2), ROWS8)
    mla_row = jnp.concatenate([pr['ckv'], pr['kr'][:, ROPE_LO:ROPE_LO + MLA_ROPE]], axis=1)
    tail = lambda a: pad_rows(a[:, None, :], TAIL)
    tnsa, tmla, tdiff, twin = tail(pr['nsarow']), tail(mla_row), tail(pr['diffrow']), tail(pr['winrow'])
    pair = np.zeros((PAST_LEN // CMP_BLOCK, N_SEL_PAD), np.float32)
    ncb = PAST_LEN // CMP_BLOCK
    pair[np.arange(ncb), np.arange(ncb) // (SEL_BLOCK // CMP_BLOCK)] = 1.0

    seq = lambda a: pl.BlockSpec((1,) + tuple(a.shape[1:]), lambda b, c, pt: (b,) + (0,) * (a.ndim - 1))
    cst = lambda a: pl.BlockSpec(tuple(np.shape(a)), lambda b, c, pt: (0,) * np.ndim(a))

    def page_spec(cache, i):
        w = cache.shape[-1]
        return pl.BlockSpec((1, 1, PAGE_SIZE, w), lambda b, c, pt: (l, pt[b, c * PAGES_PER_STEP + i], 0, 0))

    seq_ins = [qn8, qg8, qr8, qd8, gates8, tnsa, tmla, tdiff, twin, win_buf]
    consts = [sw['wukt'], sw['ngain'], sw['wuv'], lw['pe_mean'], lw['wc'], lw['cgain'], pair, dpar]
    pages = [cache_nsa] * PAGES_PER_STEP + [cache_mla] * PAGES_PER_STEP + [cache_diff] * PAGES_PER_STEP
    in_specs = [pl.BlockSpec(memory_space=pltpu.SMEM)] + [seq(a) for a in seq_ins] \
        + [page_spec(cache_nsa, i) for i in range(PAGES_PER_STEP)] \
        + [page_spec(cache_mla, i) for i in range(PAGES_PER_STEP)] \
        + [page_spec(cache_diff, i) for i in range(PAGES_PER_STEP)] \
        + [cst(a) for a in consts]
    out_shape = [jax.ShapeDtypeStruct((nb, ROWS8, LANE), F32), jax.ShapeDtypeStruct((nb, 1, MLA_HEADS * MLA_V), F32),
                 jax.ShapeDtypeStruct((nb, ROWS8, LANE), F32)]
    out_specs = [pl.BlockSpec((1, ROWS8, LANE), lambda b, c, pt: (b, 0, 0)),
                 pl.BlockSpec((1, 1, MLA_HEADS * MLA_V), lambda b, c, pt: (b, 0, 0)),
                 pl.BlockSpec((1, ROWS8, LANE), lambda b, c, pt: (b, 0, 0))]
    r8 = (ROWS8, 1)
    scratch = [pltpu.VMEM((ROWS8, MLA_KV_LORA), F32),
               pltpu.VMEM(r8, F32), pltpu.VMEM(r8, F32), pltpu.VMEM((ROWS8, LANE), F32),
               pltpu.VMEM(r8, F32), pltpu.VMEM(r8, F32), pltpu.VMEM((ROWS8, LANE), F32),
               pltpu.VMEM((PAST_LEN // CMP_BLOCK, LANE), F32),
               pltpu.VMEM((PAST_LEN + SEL_BLOCK, LANE), F32),
               pltpu.VMEM((TOP_N * SEL_BLOCK, LANE), F32)]
    grid_spec = pltpu.PrefetchScalarGridSpec(num_scalar_prefetch=1, grid=(nb, nchunk), in_specs=in_specs,
                                             out_specs=out_specs, scratch_shapes=scratch)
    onsa, omla, odiff = pl.pallas_call(
        _sample_kernel, grid_spec=grid_spec, out_shape=out_shape,
        compiler_params=pltpu.CompilerParams(dimension_semantics=("arbitrary", "arbitrary"),
                                             vmem_limit_bytes=VMEM_LIMIT),
        name="sample_paged_mixers")(page_table, table.reshape(-1), *seq_ins, *pages, *consts)
    nsa = onsa[:, :NSA_HEADS].reshape(nb, NSA_HEADS * LANE)
    dif = odiff[:, 0::2].reshape(nb, DIFF_HEADS * LANE)
    return nsa, omla[:, 0], dif, mla_row


PROMPT_TM = 256
PROMPT_FFN_TM = 512


def kernel(x_prompt, x_sample, cache_nsa, cache_mla, cache_diff, state_nsa_win, page_table, attn_norm, w_in, nsa_qk_gain, nsa_cmp_pe, nsa_cmp_w, mla_q_norm, mla_kv_norm, mla_w_uq, mla_w_ukv, mla_nope_gain, mla_rope_gain, diff_qk_gain, diff_lambda, diff_head_gain, w_out, rel_bias_table, ffn_norm, moe_w_group, moe_b_group, moe_w_router, moe_b_router, moe_w1, moe_w3, moe_w2):
    p = dict(attn_norm=attn_norm, w_in=w_in, nsa_qk_gain=nsa_qk_gain, nsa_cmp_pe=nsa_cmp_pe,
             nsa_cmp_w=nsa_cmp_w, mla_q_norm=mla_q_norm, mla_kv_norm=mla_kv_norm, mla_w_uq=mla_w_uq,
             mla_w_ukv=mla_w_ukv, mla_nope_gain=mla_nope_gain, mla_rope_gain=mla_rope_gain,
             diff_qk_gain=diff_qk_gain, diff_lambda=diff_lambda, diff_head_gain=diff_head_gain,
             w_out=w_out, ffn_norm=ffn_norm, moe_w_group=moe_w_group, moe_b_group=moe_b_group,
             moe_w_router=moe_w_router, moe_b_router=moe_b_router, moe_w1=moe_w1, moe_w3=moe_w3,
             moe_w2=moe_w2)
    assert x_prompt.shape[0] == 1 and x_sample.shape[1] == 1
    xp, xs = x_prompt[0], x_sample[:, 0]
    s, nb = xp.shape[0], xs.shape[0]
    pos_p = jnp.arange(s)
    pos_s = jnp.full((nb,), PAST_LEN, jnp.int32)
    table = rel_bias_table.astype(F32)
    wb = state_nsa_win.shape[2]
    outs = [[] for _ in range(8)]
    for l in range(w_in.shape[0]):
        lw, fw, sw, dpar = _layer_weights(l, p), _ffn_weights(l, p), _sample_weights(l, p), _diff_params(l, p)
        pp = _project(xp, pos_p, lw, tm=PROMPT_TM, exact=False, qdtype=BF16)
        nsa = _nsa_attention(pp['qnsa'], pp['gates'], pp['nsarowb'], pp['winrowb'], pp['cmp'], table)
        mla = _mla_attention(pp['qmla'], pp['kmla'], pp['vmla'])
        dif = _diff_attention(pp['qd'], pp['diffrowb'], table, dpar)
        xp = _ffn(xp, nsa, mla, dif, fw, tm=PROMPT_FFN_TM, exact=False)
        ps = _project(xs, pos_s, lw, tm=nb, exact=True, qdtype=F32)
        win_buf = state_nsa_win[l]
        nsa_s, mla_s, dif_s, mla_row_s = _sample_mixers(l, page_table, cache_nsa, cache_mla, cache_diff,
                                                        win_buf, ps, lw, sw, table, dpar)
        xs = _ffn(xs, nsa_s, mla_s, dif_s, fw, tm=nb, exact=True)
        mla_row_p = jnp.concatenate([pp['ckv'], pp['kr'][:, ROPE_LO:ROPE_LO + MLA_ROPE]], axis=1)
        new = (pp['nsarow'][None], ps['nsarow'][:, None], mla_row_p[None], mla_row_s[:, None],
               pp['diffrow'][None], ps['diffrow'][:, None], pp['winrow'][None, s - min(WINDOW, s):],
               jnp.concatenate([win_buf, ps['winrow'][:, None]], axis=1)[:, -wb:])
        for acc, a in zip(outs, new):
            acc.append(a)
    return (xp[None], xs[:, None]) + tuple(jnp.stack(a) for a in outs)
```

```python
import functools
import math

import numpy as np
import jax
import jax.numpy as jnp
from jax import lax
from jax.experimental import pallas as pl
from jax.experimental.pallas import tpu as pltpu

F32 = jnp.float32
BF16 = jnp.bfloat16
HIGHEST = lax.Precision.HIGHEST

D_MODEL = 1024
PAST_LEN = 16384
PAGE_SIZE = 128
HEAD_DIM = 64
NSA_HEADS = 6
CMP_BLOCK = 32
SEL_BLOCK = 64
TOP_N = 16
WINDOW = 512
MLA_HEADS = 6
MLA_Q_LORA = 192
MLA_KV_LORA = 128
MLA_NOPE = 64
MLA_ROPE = 32
MLA_V = 64
ROPE_THETA = 10000.0
DIFF_HEADS = 4
DIFF_KV_HEADS = 2
DIFF_QK = 32
DIFF_V = 64
REL_BUCKETS = 32
REL_MAX_DIST = 2048
N_GROUPS = 4
EXPERTS_PER_GROUP = 4
N_EXPERTS = N_GROUPS * EXPERTS_PER_GROUP
D_EXPERT = 256
RMS_EPS = 1e-6
NSA_SCALE = HEAD_DIM ** -0.5
MLA_SCALE = (MLA_NOPE + MLA_ROPE) ** -0.5
DIFF_SCALE = DIFF_QK ** -0.5
IN_SIZES = (384, 384, 18, 192, 128, 32, 256, 128, 128)
IN_WIDTH = sum(IN_SIZES)

LANE = 128
VMEM_LIMIT = 56 * 1024 * 1024

OFF_A, OFF_B, OFF_C, OFF_D, OFF_E, OFF_F, OFF_G, OFF_H = 0, 768, 1152, 1408, 1536, 2560, 2688, 2816
W_Y = 2944
N_GRP = 128
G_NSA, G_QN, G_KVN, G_DQK, G_RG, G_NG, G_ONE, G_ZERO = 0, 256, 448, 576, 704, 768, 896, 897
ROPE_LO = 64


def _layout_stage1():
    src = np.full((W_Y,), IN_WIDTH, np.int32)
    grp = np.full((W_Y,), -1, np.int32)
    gsrc = np.full((W_Y,), G_ZERO, np.int32)
    pas = np.zeros((W_Y,), np.float32)
    gsize = np.ones((N_GRP,), np.float32)
    o_q, o_kv, o_g, o_cq, o_ckv, o_kr, o_qd, o_kd, o_vd = np.cumsum((0,) + IN_SIZES[:-1])
    for h in range(NSA_HEADS):
        for d in range(HEAD_DIM):
            c = OFF_A + h * LANE + d
            src[c], grp[c], gsrc[c] = o_q + h * HEAD_DIM + d, h, G_NSA + d
        gsize[h] = HEAD_DIM
    for i in range(384):
        c = OFF_B + i
        src[c] = o_kv + i
        blk = i // HEAD_DIM
        if blk == 2:
            grp[c], gsrc[c] = 6, G_NSA + 2 * HEAD_DIM + i % HEAD_DIM
        elif blk == 4:
            grp[c], gsrc[c] = 7, G_NSA + 3 * HEAD_DIM + i % HEAD_DIM
        else:
            pas[c] = 1.0
    gsize[6] = gsize[7] = HEAD_DIM
    for i in range(MLA_Q_LORA):
        c = OFF_C + i
        src[c], grp[c], gsrc[c] = o_cq + i, 8, G_QN + i
    gsize[8] = MLA_Q_LORA
    for i in range(MLA_KV_LORA):
        c = OFF_D + i
        src[c], grp[c], gsrc[c] = o_ckv + i, 9, G_KVN + i
    gsize[9] = MLA_KV_LORA
    for g in range(2):
        for j in range(2):
            for m in range(2):
                r = g * 4 + j * 2 + m
                for d in range(DIFF_QK):
                    c = OFF_E + r * LANE + g * 64 + m * 32 + d
                    src[c] = o_qd + (g * 2 + j) * 64 + m * 32 + d
                    grp[c], gsrc[c] = 10 + r, G_DQK + m * 32 + d
                gsize[10 + r] = DIFF_QK
    for i in range(128):
        c = OFF_F + i
        gm = i // 32
        src[c], grp[c], gsrc[c] = o_kd + i, 18 + gm, G_DQK + (2 + gm % 2) * 32 + i % 32
        gsize[18 + gm] = DIFF_QK
    for i in range(128):
        c = OFF_G + i
        src[c], pas[c] = o_vd + i, 1.0
    for i in range(18):
        c = OFF_H + i
        src[c], pas[c] = o_g + i, 1.0
    for d in range(MLA_ROPE):
        c = OFF_H + ROPE_LO + d
        src[c], grp[c], gsrc[c] = o_kr + d, 22, G_RG + MLA_ROPE + d
    gsize[22] = MLA_ROPE
    return src, grp, gsrc, pas, gsize


def _layout_q_mla():
    w = MLA_HEADS * LANE
    src = np.full((w,), MLA_HEADS * (MLA_NOPE + MLA_ROPE), np.int32)
    grp = np.full((w,), -1, np.int32)
    gsrc = np.full((w,), G_ZERO, np.int32)
    gsize = np.ones((N_GRP,), np.float32)
    for h in range(MLA_HEADS):
        for t in range(MLA_NOPE + MLA_ROPE):
            c = h * LANE + t
            src[c] = h * (MLA_NOPE + MLA_ROPE) + t
            if t < MLA_NOPE:
                grp[c], gsrc[c] = 2 * h, G_NG + t
            else:
                grp[c], gsrc[c] = 2 * h + 1, G_RG + (t - MLA_NOPE)
        gsize[2 * h], gsize[2 * h + 1] = MLA_NOPE, MLA_ROPE
    return src, grp, gsrc, gsize


def _layout_kv_mla():
    wk = MLA_HEADS * LANE
    wv = MLA_HEADS * MLA_V
    src = np.full((wk + wv,), MLA_HEADS * (MLA_NOPE + MLA_V), np.int32)
    grp = np.full((wk,), -1, np.int32)
    gsrc = np.full((wk,), G_ZERO, np.int32)
    gsize = np.ones((N_GRP,), np.float32)
    for h in range(MLA_HEADS):
        for d in range(MLA_NOPE):
            c = h * LANE + d
            src[c], grp[c], gsrc[c] = h * (MLA_NOPE + MLA_V) + d, h, G_NG + MLA_NOPE + d
        gsize[h] = MLA_NOPE
        for d in range(MLA_V):
            src[wk + h * MLA_V + d] = h * (MLA_NOPE + MLA_V) + MLA_NOPE + d
    return src, grp, gsrc, gsize


def _group_mats(grp, gsize):
    w = grp.shape[0]
    g = np.zeros((w, N_GRP), np.float32)
    cols = np.nonzero(grp >= 0)[0]
    g[cols, grp[cols]] = 1.0
    return g, np.ascontiguousarray(g.T), (1.0 / gsize).reshape(1, N_GRP).astype(np.float32)


_S1 = _layout_stage1()
_S2 = _layout_q_mla()
_S3 = _layout_kv_mla()
_G1, _GT1, _INVN1 = _group_mats(_S1[1], _S1[4])
_G2, _GT2, _INVN2 = _group_mats(_S2[1], _S2[3])
_G3, _GT3, _INVN3 = _group_mats(_S3[1], _S3[3])


def _take_cols(w, src):
    wz = jnp.concatenate([w, jnp.zeros((w.shape[0], 1), w.dtype)], axis=1)
    return jnp.take(wz, jnp.asarray(src), axis=1)


def _flat_gains(nsa_gain, q_norm, kv_norm, dqk_gain, rope_gain, nope_gain):
    return jnp.concatenate([nsa_gain.reshape(-1), q_norm, kv_norm, dqk_gain.reshape(-1),
                            rope_gain.reshape(-1), nope_gain.reshape(-1),
                            jnp.ones((1,), F32), jnp.zeros((1,), F32)]).astype(F32)


def _rope_tables(pos):
    half = MLA_ROPE // 2
    inv = ROPE_THETA ** (-jnp.arange(half, dtype=jnp.float32) / half)
    ang = pos.astype(jnp.float32)[:, None] * inv[None, :]
    cos, sin = jnp.cos(ang), jnp.sin(ang)
    t = pos.shape[0]
    one = jnp.ones((t, ROPE_LO), F32)
    z16 = jnp.zeros((t, half), F32)
    z32 = jnp.zeros((t, LANE - ROPE_LO - MLA_ROPE), F32)
    z64 = jnp.zeros((t, ROPE_LO), F32)
    c_tab = jnp.concatenate([one, cos, cos, z32], axis=1)
    s1_tab = jnp.concatenate([z64, -sin, z16, z32], axis=1)
    s2_tab = jnp.concatenate([z64, z16, sin, z32], axis=1)
    return c_tab, s1_tab, s2_tab


def _mm(a, b):
    return jnp.dot(a.astype(BF16), b.astype(BF16), preferred_element_type=F32)


def _mm_nt(a, b):
    return lax.dot_general(a.astype(BF16), b.astype(BF16), (((1,), (1,)), ((), ())),
                           preferred_element_type=F32)


def _segnorm(y, g_ref, gt_ref, invn_ref, gain_ref, pass_row):
    ssq = jnp.dot(y * y, g_ref[...], precision=HIGHEST, preferred_element_type=F32)
    inv = lax.rsqrt(ssq * invn_ref[...] + RMS_EPS)
    fac = jnp.dot(inv, gt_ref[...], precision=HIGHEST, preferred_element_type=F32) * gain_ref[...]
    if pass_row is not None:
        fac = fac + pass_row
    return y * fac


def _rope_slab(x, c_tab, s1_tab, s2_tab):
    return (x * c_tab + pltpu.roll(x, LANE - MLA_ROPE // 2, 1) * s1_tab
            + pltpu.roll(x, MLA_ROPE // 2, 1) * s2_tab)


def _proj_kernel(x_ref, an_ref, w1_ref, g1_ref, gt1_ref, invn1_ref, gain1_ref, pass1_ref,
                 rc_ref, rs1_ref, rs2_ref, wuq_ref, g2_ref, gt2_ref, invn2_ref, gain2_ref,
                 wukv_ref, g3_ref, gt3_ref, invn3_ref, gain3_ref,
                 pool_ref, pem_ref, wc_ref, cgain_ref,
                 qnsa_ref, gates_ref, nsarow_ref, winrow_ref, qmla_ref, ckv_ref, kr_ref,
                 qd_ref, diffrow_ref, kmla_ref, vmla_ref, nsarowb_ref, winrowb_ref, diffrowb_ref,
                 cmp_ref, *, fold_scale):
    s_nsa, s_mla, s_diff = (NSA_SCALE, MLA_SCALE, DIFF_SCALE) if fold_scale else (1.0, 1.0, 1.0)
    x = x_ref[...]
    h = x * lax.rsqrt(jnp.mean(x * x, axis=-1, keepdims=True) + RMS_EPS) * an_ref[...]
    y = _mm(h, w1_ref[...])
    yn = _segnorm(y, g1_ref, gt1_ref, invn1_ref, gain1_ref, pass1_ref[...])
    rc, rs1, rs2 = rc_ref[...], rs1_ref[...], rs2_ref[...]

    qnsa_ref[...] = (yn[:, OFF_A:OFF_A + 768] * s_nsa).astype(qnsa_ref.dtype)
    nsa_row = yn[:, OFF_B:OFF_B + 256]
    nsarow_ref[...] = nsa_row
    nsarowb_ref[...] = nsa_row[:, LANE:2 * LANE].astype(BF16)
    win_row = yn[:, OFF_B + 256:OFF_B + 384]
    winrow_ref[...] = win_row
    winrowb_ref[...] = win_row.astype(BF16)
    ckv = yn[:, OFF_D:OFF_D + 128]
    ckv_ref[...] = ckv
    qd_ref[...] = (yn[:, OFF_E:OFF_E + 1024] * s_diff).astype(qd_ref.dtype)
    diff_row = yn[:, OFF_F:OFF_F + 256]
    diffrow_ref[...] = diff_row
    diffrowb_ref[...] = diff_row.astype(BF16)
    y_h = y[:, OFF_H:OFF_H + 128]
    gates_ref[...] = jax.nn.sigmoid(y_h)
    kr = _rope_slab(yn[:, OFF_H:OFF_H + 128], rc, rs1, rs2)
    lane = lax.broadcasted_iota(jnp.int32, kr.shape, 1)
    kr = jnp.where((lane >= ROPE_LO) & (lane < ROPE_LO + MLA_ROPE), kr, 0.0)
    kr_ref[...] = kr

    qf = _mm(yn[:, OFF_C:OFF_C + 256], wuq_ref[...])
    qf = _segnorm(qf, g2_ref, gt2_ref, invn2_ref, gain2_ref, None)
    for hh in range(MLA_HEADS):
        sl = slice(hh * LANE, (hh + 1) * LANE)
        qmla_ref[:, sl] = (_rope_slab(qf[:, sl], rc, rs1, rs2) * s_mla).astype(qmla_ref.dtype)

    kv = _mm(ckv, wukv_ref[...])
    kn = _segnorm(kv[:, :768], g3_ref, gt3_ref, invn3_ref, gain3_ref, None)
    for hh in range(MLA_HEADS):
        sl = slice(hh * LANE, (hh + 1) * LANE)
        kmla_ref[:, sl] = (kn[:, sl] + kr).astype(kmla_ref.dtype)
    vmla_ref[...] = kv[:, 768:].astype(vmla_ref.dtype)

    pooled = jnp.dot(pool_ref[...], nsa_row[:, 0:128], precision=HIGHEST,
                     preferred_element_type=F32) + pem_ref[...]
    cm = jnp.dot(pooled, wc_ref[...], precision=HIGHEST, preferred_element_type=F32)
    lane_c = lax.broadcasted_iota(jnp.int32, cm.shape, 1)
    is_k = lane_c < HEAD_DIM
    ms = jnp.sum(jnp.where(is_k, cm * cm, 0.0), axis=-1, keepdims=True) * (1.0 / HEAD_DIM)
    cmp_ref[...] = jnp.where(is_k, cm * lax.rsqrt(ms + RMS_EPS) * cgain_ref[...], cm)


def _const_spec(shape):
    return pl.BlockSpec(shape, lambda i: (0,) * len(shape))


def _project(x, pos, lw, *, tm, fold_scale, qdtype):
    t = x.shape[0]
    assert t % tm == 0 and tm % CMP_BLOCK == 0
    nt = t // tm
    rc, rs1, rs2 = _rope_tables(pos)
    wdt = BF16
    pool = np.zeros((tm // CMP_BLOCK, tm), np.float32)
    for r in range(tm // CMP_BLOCK):
        pool[r, r * CMP_BLOCK:(r + 1) * CMP_BLOCK] = 1.0 / CMP_BLOCK
    ins = [x, lw['attn_norm'], lw['w1'].astype(wdt), _G1, _GT1, _INVN1, lw['gain1'], lw['pass1'],
           rc, rs1, rs2, lw['wuq'].astype(wdt), _G2, _GT2, _INVN2, lw['gain2'],
           lw['wukv'].astype(wdt), _G3, _GT3, _INVN3, lw['gain3'],
           pool, lw['pe_mean'], lw['wc'], lw['cgain']]
    tok = lambda w: pl.BlockSpec((tm, w), lambda i: (i, 0))
    in_specs = [tok(D_MODEL)] + [_const_spec(tuple(np.shape(a))) for a in ins[1:8]] \
        + [tok(LANE)] * 3 + [_const_spec(tuple(np.shape(a))) for a in ins[11:]]
    widths = dict(qnsa=(768, qdtype), gates=(128, F32), nsarow=(256, F32), winrow=(128, F32),
                  qmla=(768, qdtype), ckv=(128, F32), kr=(128, F32), qd=(1024, qdtype),
                  diffrow=(256, F32), kmla=(768, qdtype), vmla=(384, qdtype),
                  nsarowb=(128, BF16), winrowb=(128, BF16), diffrowb=(256, BF16))
    out_shape = [jax.ShapeDtypeStruct((t, w), dt) for (w, dt) in widths.values()]
    out_specs = [tok(w) for (w, _) in widths.values()]
    out_shape.append(jax.ShapeDtypeStruct((t // CMP_BLOCK, LANE), F32))
    out_specs.append(pl.BlockSpec((tm // CMP_BLOCK, LANE), lambda i: (i, 0)))
    outs = pl.pallas_call(
        functools.partial(_proj_kernel, fold_scale=fold_scale),
        grid=(nt,), in_specs=in_specs, out_specs=out_specs, out_shape=out_shape,
        compiler_params=pltpu.CompilerParams(dimension_semantics=("arbitrary",),
                                             vmem_limit_bytes=VMEM_LIMIT),
        name="token_projection",
    )(*ins)
    res = dict(zip(list(widths.keys()) + ['cmp'], outs))
    return res


def _layer_weights(l, p):
    flat = _flat_gains(p['nsa_qk_gain'][l], p['mla_q_norm'][l], p['mla_kv_norm'][l],
                       p['diff_qk_gain'][l], p['mla_rope_gain'][l], p['mla_nope_gain'][l])
    lw = {}
    lw['attn_norm'] = p['attn_norm'][l].reshape(1, D_MODEL)
    lw['w1'] = _take_cols(p['w_in'][l], _S1[0])
    lw['gain1'] = jnp.take(flat, jnp.asarray(_S1[2])).reshape(1, W_Y)
    lw['pass1'] = jnp.asarray(_S1[3]).reshape(1, W_Y)
    wuq = _take_cols(p['mla_w_uq'][l], _S2[0])
    lw['wuq'] = jnp.concatenate([wuq, jnp.zeros((256 - MLA_Q_LORA, wuq.shape[1]), F32)], axis=0)
    lw['gain2'] = jnp.take(flat, jnp.asarray(_S2[2])).reshape(1, -1)
    lw['wukv'] = _take_cols(p['mla_w_ukv'][l], _S3[0])
    lw['gain3'] = jnp.take(flat, jnp.asarray(_S3[2])).reshape(1, -1)
    pe = p['nsa_cmp_pe'][l]
    lw['pe_mean'] = jnp.concatenate([jnp.mean(pe[0], axis=0), jnp.mean(pe[1], axis=0)]).reshape(1, LANE)
    wk, wv = p['nsa_cmp_w'][l][0], p['nsa_cmp_w'][l][1]
    z = jnp.zeros((HEAD_DIM, HEAD_DIM), F32)
    lw['wc'] = jnp.concatenate([jnp.concatenate([wk, z], axis=1), jnp.concatenate([z, wv], axis=1)], axis=0)
    lw['cgain'] = jnp.concatenate([p['nsa_qk_gain'][l][1], jnp.ones((HEAD_DIM,), F32)]).reshape(1, LANE)
    return lw


def _bucket_np(n):
    n = np.asarray(n)
    nf = np.maximum(n, 1).astype(np.float64)
    large = 16 + (np.log(nf / 16.0) / math.log(REL_MAX_DIST / 16.0) * 16.0).astype(np.int64)
    return np.where(n < 16, n, np.minimum(large, REL_BUCKETS - 1)).astype(np.int32)


_BUCKET_LB = [int(np.nonzero(_bucket_np(np.arange(4096)) >= b)[0][0]) for b in range(REL_BUCKETS)]
FAR_DIST = _BUCKET_LB[-1]
NEG = -1e30

TQ = 256
TK = 512
BAND_ROWS = 128
BAND_C = 2432
BAND_W = BAND_C + TK
assert BAND_C - (TK - 1) >= FAR_DIST and BAND_C % LANE == 0 and TQ % BAND_ROWS == 0

_BAND_BID = _bucket_np(np.maximum(
    BAND_C + np.arange(BAND_ROWS)[:, None] - np.arange(BAND_W)[None, :], 0))


def _bias_from_dist(dist, tab_ref, col):
    b = jnp.full(dist.shape, tab_ref[0 * _TAB_COLS + col], F32)
    for k in range(1, REL_BUCKETS):
        b = jnp.where(dist >= _BUCKET_LB[k], tab_ref[k * _TAB_COLS + col], b)
    return b


_TAB_COLS = NSA_HEADS + DIFF_HEADS


def _build_band(band_ref, bid_ref, tab_ref, col0, nheads):
    bid = bid_ref[...]
    for h in range(nheads):
        b = jnp.full(bid.shape, tab_ref[(REL_BUCKETS - 1) * _TAB_COLS + col0 + h], F32)
        for k in range(REL_BUCKETS - 2, -1, -1):
            b = jnp.where(bid == k, tab_ref[k * _TAB_COLS + col0 + h], b)
        band_ref[h] = b


def _band_tile(band_ref, h, delta):
    parts = []
    for rb in range(TQ // BAND_ROWS):
        c0 = BAND_C - jnp.minimum(delta + rb * BAND_ROWS, BAND_C)
        parts.append(band_ref[h, :, pl.ds(pl.multiple_of(c0, LANE), TK)])
    return jnp.concatenate(parts, axis=0)


def _causal_pairs(s):
    qi, ki = [], []
    for a in range(s // TQ):
        for b in range((a * TQ + TQ - 1) // TK + 1):
            qi.append(a)
            ki.append(b)
    return np.asarray(qi, np.int32), np.asarray(ki, np.int32)


def _softmax_step(s, m_ref, l_ref, acc_ref, idx, v, keep=None):
    m_old = m_ref[idx]
    m_new = jnp.maximum(m_old, jnp.max(s, axis=-1, keepdims=True))
    alpha = jnp.exp(m_old - m_new)
    p = jnp.exp(s - m_new)
    if keep is not None:
        p = jnp.where(keep, p, 0.0)
    l_ref[idx] = alpha * l_ref[idx] + jnp.sum(p, axis=-1, keepdims=True)
    m_ref[idx] = m_new
    acc_ref[idx] = alpha * acc_ref[idx] + jnp.dot(p.astype(BF16), v, preferred_element_type=F32)


def _init_softmax(m_ref, l_ref, acc_ref):
    m_ref[...] = jnp.full(m_ref.shape, NEG, F32)
    l_ref[...] = jnp.zeros(l_ref.shape, F32)
    acc_ref[...] = jnp.zeros(acc_ref.shape, F32)


def _tile_positions(q0, k0):
    rows = q0 + lax.broadcasted_iota(jnp.int32, (TQ, TK), 0)
    cols = k0 + lax.broadcasted_iota(jnp.int32, (TQ, TK), 1)
    return rows, cols


def _mla_attn_kernel(qt_ref, kt_ref, q_ref, k_ref, v_ref, o_ref, m_sc, l_sc, acc_sc):
    step = pl.program_id(0)
    qi, ki = qt_ref[step], kt_ref[step]
    q0, k0 = qi * TQ, ki * TK

    @pl.when(ki == 0)
    def _():
        _init_softmax(m_sc, l_sc, acc_sc)

    def update(masked):
        if masked:
            rows, cols = _tile_positions(q0, k0)
            keep = cols <= rows
        for h in range(MLA_HEADS):
            sl = slice(h * LANE, (h + 1) * LANE)
            s = _mm_nt(q_ref[:, sl], k_ref[:, sl])
            if masked:
                s = jnp.where(keep, s, NEG)
            pr = slice((h // 2) * LANE, (h // 2 + 1) * LANE)
            _softmax_step(s, m_sc, l_sc, acc_sc, h, v_ref[:, pr])

    on_diag = k0 + TK - 1 > q0
    pl.when(on_diag)(lambda: update(True))
    pl.when(jnp.logical_not(on_diag))(lambda: update(False))

    @pl.when(ki == (q0 + TQ - 1) // TK)
    def _():
        lane = lax.broadcasted_iota(jnp.int32, (TQ, LANE), 1)
        for pi in range(MLA_HEADS // 2):
            lo = acc_sc[2 * pi] / l_sc[2 * pi]
            hi = acc_sc[2 * pi + 1] / l_sc[2 * pi + 1]
            o_ref[:, pi * LANE:(pi + 1) * LANE] = jnp.where(lane < MLA_V, lo, hi)


def _mla_attention(q, k, v):
    s = q.shape[0]
    qt, kt = _causal_pairs(s)
    grid_spec = pltpu.PrefetchScalarGridSpec(
        num_scalar_prefetch=2, grid=(qt.shape[0],),
        in_specs=[pl.BlockSpec((TQ, 768), lambda i, a, b: (a[i], 0)),
                  pl.BlockSpec((TK, 768), lambda i, a, b: (b[i], 0)),
                  pl.BlockSpec((TK, 384), lambda i, a, b: (b[i], 0))],
        out_specs=pl.BlockSpec((TQ, 384), lambda i, a, b: (a[i], 0)),
        scratch_shapes=[pltpu.VMEM((MLA_HEADS, TQ, 1), F32), pltpu.VMEM((MLA_HEADS, TQ, 1), F32),
                        pltpu.VMEM((MLA_HEADS, TQ, LANE), F32)])
    return pl.pallas_call(
        _mla_attn_kernel, grid_spec=grid_spec, out_shape=jax.ShapeDtypeStruct((s, 384), F32),
        compiler_params=pltpu.CompilerParams(dimension_semantics=("arbitrary",),
                                             vmem_limit_bytes=VMEM_LIMIT),
        name="mla_prompt_attention")(qt, kt, q, k, v)


N_DMAPS = 2 * DIFF_HEADS


def _diff_attn_kernel(qt_ref, kt_ref, tab_ref, bid_ref, q_ref, kv_ref, par_ref, o_ref,
                      band_sc, m_sc, l_sc, acc_sc):
    step = pl.program_id(0)
    qi, ki = qt_ref[step], kt_ref[step]
    q0, k0 = qi * TQ, ki * TK

    @pl.when(step == 0)
    def _():
        _build_band(band_sc, bid_ref, tab_ref, NSA_HEADS, DIFF_HEADS)

    @pl.when(ki == 0)
    def _():
        _init_softmax(m_sc, l_sc, acc_sc)

    def update(masked):
        if masked:
            rows, cols = _tile_positions(q0, k0)
            keep = cols <= rows
        k = kv_ref[:, 0:LANE]
        v = kv_ref[:, LANE:2 * LANE]
        for h in range(DIFF_HEADS):
            bias = _band_tile(band_sc, h, q0 - k0)
            for m in range(2):
                r = 2 * h + m
                s = _mm_nt(q_ref[:, r * LANE:(r + 1) * LANE], k) + bias
                if masked:
                    s = jnp.where(keep, s, NEG)
                _softmax_step(s, m_sc, l_sc, acc_sc, r, v)

    on_diag = k0 + TK - 1 > q0
    pl.when(on_diag)(lambda: update(True))
    pl.when(jnp.logical_not(on_diag))(lambda: update(False))

    @pl.when(ki == (q0 + TQ - 1) // TK)
    def _():
        lane = lax.broadcasted_iota(jnp.int32, (TQ, LANE), 1)
        gain = par_ref[0:1, :]
        lam = par_ref[1:2, :]
        for h in range(DIFF_HEADS):
            g = h // 2
            o = acc_sc[2 * h] / l_sc[2 * h] - lam * (acc_sc[2 * h + 1] / l_sc[2 * h + 1])
            o = jnp.where((lane >= g * DIFF_V) & (lane < (g + 1) * DIFF_V), o, 0.0)
            ms = jnp.sum(o * o, axis=-1, keepdims=True) * (1.0 / DIFF_V)
            o_ref[:, h * LANE:(h + 1) * LANE] = o * lax.rsqrt(ms + RMS_EPS) * gain


def _diff_params(l, p):
    lam_init = 0.8 - 0.6 * math.exp(-0.3 * l)
    lv = p['diff_lambda'][l].astype(F32)
    lam = jnp.exp(jnp.sum(lv[0] * lv[1])) - jnp.exp(jnp.sum(lv[2] * lv[3])) + lam_init
    gain = jnp.tile(p['diff_head_gain'][l].astype(F32), 2) * (1.0 - lam_init)
    return jnp.concatenate([gain.reshape(1, LANE), jnp.full((1, LANE), lam, F32),
                            jnp.zeros((6, LANE), F32)], axis=0)


def _diff_attention(q, kv, table, dpar):
    s = q.shape[0]
    qt, kt = _causal_pairs(s)
    smem = pl.BlockSpec(memory_space=pltpu.SMEM)
    grid_spec = pltpu.PrefetchScalarGridSpec(
        num_scalar_prefetch=2, grid=(qt.shape[0],),
        in_specs=[smem,
                  pl.BlockSpec((BAND_ROWS, BAND_W), lambda i, a, b: (0, 0)),
                  pl.BlockSpec((TQ, 1024), lambda i, a, b: (a[i], 0)),
                  pl.BlockSpec((TK, 256), lambda i, a, b: (b[i], 0)),
                  pl.BlockSpec((8, LANE), lambda i, a, b: (0, 0))],
        out_specs=pl.BlockSpec((TQ, 512), lambda i, a, b: (a[i], 0)),
        scratch_shapes=[pltpu.VMEM((DIFF_HEADS, BAND_ROWS, BAND_W), F32),
                        pltpu.VMEM((N_DMAPS, TQ, 1), F32), pltpu.VMEM((N_DMAPS, TQ, 1), F32),
                        pltpu.VMEM((N_DMAPS, TQ, LANE), F32)])
    return pl.pallas_call(
        _diff_attn_kernel, grid_spec=grid_spec, out_shape=jax.ShapeDtypeStruct((s, 512), F32),
        compiler_params=pltpu.CompilerParams(dimension_semantics=("arbitrary",),
                                             vmem_limit_bytes=VMEM_LIMIT),
        name="diff_prompt_attention")(qt, kt, table.reshape(-1), _BAND_BID, q, kv, dpar)


SEL_FORCED = 1e30
SEL_INVALID = -1e30
SEL_TAKEN = -2e30
SEL_SHIFT = SEL_BLOCK.bit_length() - 1
assert 1 << SEL_SHIFT == SEL_BLOCK


def _select_blocks(score, blk):
    sel = jnp.zeros(score.shape, jnp.bool_)
    for _ in range(TOP_N):
        m = jnp.max(score, axis=-1, keepdims=True)
        idx = jnp.min(jnp.where(score == m, blk, 1 << 20), axis=-1, keepdims=True)
        hit = blk == idx
        sel = sel | hit
        score = jnp.where(hit, SEL_TAKEN, score)
    return sel


def _masked_softmax_rows(s, keep):
    s = jnp.where(keep, s, NEG)
    m = jnp.max(s, axis=-1, keepdims=True)
    e = jnp.where(keep, jnp.exp(s - m), 0.0)
    z = jnp.sum(e, axis=-1, keepdims=True)
    return e / jnp.where(z > 0.0, z, 1.0)


def _nsa_attn_kernel(qt_ref, kt_ref, tab_ref, bid_ref, q_ref, gates_ref, ksel_ref, kwin_ref, cmp_ref,
                     pair_ref, o_ref, band_sc, sel_sc, oc_sc, m_sc, l_sc, acc_sc, mw_sc, lw_sc, accw_sc):
    step = pl.program_id(0)
    qi, ki = qt_ref[step], kt_ref[step]
    q0, k0 = qi * TQ, ki * TK
    nc = cmp_ref.shape[0]
    ns = pair_ref.shape[1]

    @pl.when(step == 0)
    def _():
        _build_band(band_sc, bid_ref, tab_ref, 0, NSA_HEADS)

    @pl.when(ki == 0)
    def _():
        _init_softmax(m_sc, l_sc, acc_sc)
        _init_softmax(mw_sc, lw_sc, accw_sc)
        cm = cmp_ref[...].astype(BF16)
        qpos = q0 + lax.broadcasted_iota(jnp.int32, (TQ, nc), 0)
        cend = (lax.broadcasted_iota(jnp.int32, (TQ, nc), 1) + 1) * CMP_BLOCK - 1
        dist = qpos - cend
        keep = dist >= 0
        imp = jnp.zeros((TQ, nc), F32)
        for h in range(NSA_HEADS):
            s = _mm_nt(q_ref[:, h * LANE:(h + 1) * LANE], cm)
            p = _masked_softmax_rows(s + _bias_from_dist(dist, tab_ref, h), keep)
            oc_sc[h] = jnp.dot(p.astype(BF16), cm, preferred_element_type=F32)
            imp = imp + p
        imp = jnp.dot(imp, pair_ref[...], precision=HIGHEST, preferred_element_type=F32)
        blk = lax.broadcasted_iota(jnp.int32, (TQ, ns), 1)
        qp = q0 + lax.broadcasted_iota(jnp.int32, (TQ, ns), 0)
        cur = qp // SEL_BLOCK
        forced = (blk == 0) | (blk == cur) | (blk == cur - 1)
        valid = blk * SEL_BLOCK <= qp
        score = jnp.where(valid, jnp.where(forced, SEL_FORCED, imp), SEL_INVALID)
        sel_sc[...] = _select_blocks(score, blk).astype(F32)

    rows, cols = _tile_positions(q0, k0)
    dist = rows - cols
    eb = lax.broadcasted_iota(jnp.int32, (ns, TK), 0)
    ek = k0 + lax.broadcasted_iota(jnp.int32, (ns, TK), 1)
    expand = jnp.where(eb == jnp.right_shift(ek, SEL_SHIFT), 1.0, 0.0).astype(BF16)
    chosen = jnp.dot(sel_sc[...].astype(BF16), expand, preferred_element_type=F32)
    keep_s = (chosen > 0.5) & (dist >= 0)
    in_win = k0 + TK - 1 >= q0 - (WINDOW - 1)
    kv = ksel_ref[...]
    for h in range(NSA_HEADS):
        s = _mm_nt(q_ref[:, h * LANE:(h + 1) * LANE], kv) + _band_tile(band_sc, h, q0 - k0)
        _softmax_step(jnp.where(keep_s, s, NEG), m_sc, l_sc, acc_sc, h, kv)

    @pl.when(in_win)
    def _():
        keep_w = (dist >= 0) & (dist < WINDOW)
        kw = kwin_ref[...]
        for h in range(NSA_HEADS):
            s = _mm_nt(q_ref[:, h * LANE:(h + 1) * LANE], kw) + _band_tile(band_sc, h, q0 - k0)
            _softmax_step(jnp.where(keep_w, s, NEG), mw_sc, lw_sc, accw_sc, h, kw, keep=keep_w)

    @pl.when(ki == (q0 + TQ - 1) // TK)
    def _():
        lane = lax.broadcasted_iota(jnp.int32, (TQ, LANE), 1)
        for h in range(NSA_HEADS):
            o = (gates_ref[:, 3 * h:3 * h + 1] * oc_sc[h]
                 + gates_ref[:, 3 * h + 1:3 * h + 2] * (acc_sc[h] / l_sc[h])
                 + gates_ref[:, 3 * h + 2:3 * h + 3] * (accw_sc[h] / lw_sc[h]))
            o_ref[:, h * LANE:(h + 1) * LANE] = jnp.where(lane >= HEAD_DIM, o, 0.0)


def _first_window_tile(q0):
    return jnp.maximum(q0 - (WINDOW - 1), 0) // TK


def _nsa_attention(q, gates, nsarow_b, winrow_b, cmp, table):
    s = q.shape[0]
    nc, ns = s // CMP_BLOCK, s // SEL_BLOCK
    qt, kt = _causal_pairs(s)
    pair = np.zeros((nc, ns), np.float32)
    pair[np.arange(nc), np.arange(nc) // (SEL_BLOCK // CMP_BLOCK)] = 1.0
    smem = pl.BlockSpec(memory_space=pltpu.SMEM)
    hq = (NSA_HEADS, TQ)
    grid_spec = pltpu.PrefetchScalarGridSpec(
        num_scalar_prefetch=2, grid=(qt.shape[0],),
        in_specs=[smem,
                  pl.BlockSpec((BAND_ROWS, BAND_W), lambda i, a, b: (0, 0)),
                  pl.BlockSpec((TQ, 768), lambda i, a, b: (a[i], 0)),
                  pl.BlockSpec((TQ, LANE), lambda i, a, b: (a[i], 0)),
                  pl.BlockSpec((TK, LANE), lambda i, a, b: (b[i], 0)),
                  pl.BlockSpec((TK, LANE), lambda i, a, b: (jnp.maximum(b[i], _first_window_tile(a[i] * TQ)), 0)),
                  pl.BlockSpec((nc, LANE), lambda i, a, b: (0, 0)),
                  pl.BlockSpec((nc, ns), lambda i, a, b: (0, 0))],
        out_specs=pl.BlockSpec((TQ, 768), lambda i, a, b: (a[i], 0)),
        scratch_shapes=[pltpu.VMEM((NSA_HEADS, BAND_ROWS, BAND_W), F32),
                        pltpu.VMEM((TQ, ns), F32), pltpu.VMEM(hq + (LANE,), F32),
                        pltpu.VMEM(hq + (1,), F32), pltpu.VMEM(hq + (1,), F32), pltpu.VMEM(hq + (LANE,), F32),
                        pltpu.VMEM(hq + (1,), F32), pltpu.VMEM(hq + (1,), F32), pltpu.VMEM(hq + (LANE,), F32)])
    return pl.pallas_call(
        _nsa_attn_kernel, grid_spec=grid_spec, out_shape=jax.ShapeDtypeStruct((s, 768), F32),
        compiler_params=pltpu.CompilerParams(dimension_semantics=("arbitrary",),
                                             vmem_limit_bytes=VMEM_LIMIT),
        name="nsa_prompt_attention")(qt, kt, table.reshape(-1), _BAND_BID, q, gates, nsarow_b, winrow_b,
                                     cmp, pair)


ROUTE_E0 = N_GROUPS


def _ffn_kernel(x_ref, nsa_ref, mla_ref, dif_ref, won_ref, wom_ref, wod_ref, fn_ref, wr_ref, br_ref,
                w1_ref, w3_ref, w2_ref, y_ref, xn_sc, hb_sc, comb_sc, acc_sc):
    e = pl.program_id(1)
    tm = x_ref.shape[0]
    lane = lax.broadcasted_iota(jnp.int32, (tm, LANE), 1)

    @pl.when(e == 0)
    def _():
        attn = (_mm(nsa_ref[...], won_ref[...]) + _mm(mla_ref[...], wom_ref[...])
                + _mm(dif_ref[...], wod_ref[...]))
        xn = x_ref[...] + attn
        xn_sc[...] = xn
        h = xn * lax.rsqrt(jnp.mean(xn * xn, axis=-1, keepdims=True) + RMS_EPS) * fn_ref[...]
        hb = h.astype(BF16)
        hb_sc[...] = hb
        logits = jnp.dot(hb, wr_ref[...], preferred_element_type=F32) + br_ref[...]
        big = 1 << 20
        is_g = lane < N_GROUPS
        gl = jnp.where(is_g, logits, NEG)
        gmax = jnp.max(gl, axis=-1, keepdims=True)
        gidx = jnp.min(jnp.where(gl == gmax, lane, big), axis=-1, keepdims=True)
        p_g = 1.0 / jnp.sum(jnp.where(is_g, jnp.exp(gl - gmax), 0.0), axis=-1, keepdims=True)
        in_grp = (lane >= ROUTE_E0) & (lane < ROUTE_E0 + N_EXPERTS) \
            & ((lane - ROUTE_E0) // EXPERTS_PER_GROUP == gidx)
        e1 = jnp.where(in_grp, logits, NEG)
        v1 = jnp.max(e1, axis=-1, keepdims=True)
        i1 = jnp.min(jnp.where(e1 == v1, lane, big), axis=-1, keepdims=True)
        e2 = jnp.where(lane == i1, NEG, e1)
        v2 = jnp.max(e2, axis=-1, keepdims=True)
        i2 = jnp.min(jnp.where(e2 == v2, lane, big), axis=-1, keepdims=True)
        t = jnp.exp(v2 - v1)
        w_top = p_g / (1.0 + t)
        comb_sc[...] = jnp.where(lane == i1, w_top, 0.0) + jnp.where(lane == i2, w_top * t, 0.0)
        acc_sc[...] = jnp.zeros(acc_sc.shape, F32)

    hb = hb_sc[...]
    a = jnp.dot(hb, w1_ref[0], preferred_element_type=F32)
    b = jnp.dot(hb, w3_ref[0], preferred_element_type=F32)
    cw = jnp.sum(jnp.where(lane == ROUTE_E0 + e, comb_sc[...], 0.0), axis=-1, keepdims=True)
    act = a * jax.nn.sigmoid(a) * b * cw
    acc_sc[...] += jnp.dot(act.astype(BF16), w2_ref[0], preferred_element_type=F32)

    @pl.when(e == N_EXPERTS - 1)
    def _():
        y_ref[...] = xn_sc[...] + acc_sc[...]


def _ffn_weights(l, p):
    wo = p['w_out'][l]
    z = wo.shape[1]
    rows_n = np.full((NSA_HEADS * LANE,), wo.shape[0], np.int32)
    for h in range(NSA_HEADS):
        rows_n[h * LANE + HEAD_DIM + np.arange(HEAD_DIM)] = h * HEAD_DIM + np.arange(HEAD_DIM)
    rows_d = np.full((DIFF_HEADS * LANE,), wo.shape[0], np.int32)
    for h in range(DIFF_HEADS):
        rows_d[h * LANE + (h // 2) * DIFF_V + np.arange(DIFF_V)] = 768 + h * DIFF_V + np.arange(DIFF_V)
    woz = jnp.concatenate([wo, jnp.zeros((1, z), wo.dtype)], axis=0)
    fw = {}
    fw['won'] = jnp.take(woz, jnp.asarray(rows_n), axis=0)
    fw['wom'] = wo[384:768]
    fw['wod'] = jnp.take(woz, jnp.asarray(rows_d), axis=0)
    fw['fnorm'] = p['ffn_norm'][l].reshape(1, D_MODEL)
    wrt = jnp.transpose(p['moe_w_router'][l], (1, 0, 2)).reshape(D_MODEL, N_EXPERTS)
    pad = jnp.zeros((D_MODEL, LANE - N_GROUPS - N_EXPERTS), F32)
    fw['wr'] = jnp.concatenate([p['moe_w_group'][l], wrt, pad], axis=1).astype(BF16)
    fw['br'] = jnp.concatenate([p['moe_b_group'][l], p['moe_b_router'][l].reshape(-1),
                                jnp.zeros((LANE - N_GROUPS - N_EXPERTS,), F32)]).reshape(1, LANE)
    fw['w1'] = p['moe_w1'][l].astype(BF16)
    fw['w3'] = p['moe_w3'][l].astype(BF16)
    fw['w2'] = p['moe_w2'][l].astype(BF16)
    return fw


def _ffn(x, nsa, mla, dif, fw, *, tm):
    t = x.shape[0]
    assert t % tm == 0
    tok = lambda w: pl.BlockSpec((tm, w), lambda i, e: (i, 0))
    cst = lambda a: pl.BlockSpec(tuple(a.shape), lambda i, e: (0,) * a.ndim)
    won, wom, wod = fw['won'].astype(BF16), fw['wom'].astype(BF16), fw['wod'].astype(BF16)
    in_specs = [tok(D_MODEL), tok(768), tok(384), tok(512), cst(won), cst(wom), cst(wod),
                cst(fw['fnorm']), cst(fw['wr']), cst(fw['br']),
                pl.BlockSpec((1, D_MODEL, D_EXPERT), lambda i, e: (e, 0, 0)),
                pl.BlockSpec((1, D_MODEL, D_EXPERT), lambda i, e: (e, 0, 0)),
                pl.BlockSpec((1, D_EXPERT, D_MODEL), lambda i, e: (e, 0, 0))]
    return pl.pallas_call(
        _ffn_kernel,
        grid=(t // tm, N_EXPERTS), in_specs=in_specs, out_specs=tok(D_MODEL),
        out_shape=jax.ShapeDtypeStruct((t, D_MODEL), F32),
        scratch_shapes=[pltpu.VMEM((tm, D_MODEL), F32), pltpu.VMEM((tm, D_MODEL), BF16),
                        pltpu.VMEM((tm, LANE), F32), pltpu.VMEM((tm, D_MODEL), F32)],
        compiler_params=pltpu.CompilerParams(dimension_semantics=("arbitrary", "arbitrary"),
                                             vmem_limit_bytes=VMEM_LIMIT),
        name="outproj_moe")(x, nsa, mla, dif, won, wom, wod, fw['fnorm'], fw['wr'], fw['br'],
                            fw['w1'], fw['w3'], fw['w2'])


PAGES_PER_STEP = 16
CHUNK = PAGES_PER_STEP * PAGE_SIZE
TAIL = 8
ROWS8 = 8
N_SEL_PAD = 384


def _bias_rows(dist_row, tab_ref, col0, nheads, rows_per_head):
    w = dist_row.shape[1]
    masks = [dist_row >= _BUCKET_LB[k] for k in range(1, REL_BUCKETS)]
    row = lax.broadcasted_iota(jnp.int32, (ROWS8, w), 0)
    out = jnp.zeros((ROWS8, w), F32)
    for h in range(nheads):
        b = jnp.full(dist_row.shape, tab_ref[col0 + h], F32)
        for k in range(1, REL_BUCKETS):
            b = jnp.where(masks[k - 1], tab_ref[k * _TAB_COLS + col0 + h], b)
        out = jnp.where(row // rows_per_head == h, b, out)
    return out


def _far_bias_rows(tab_ref, col0, nheads, rows_per_head, w):
    row = lax.broadcasted_iota(jnp.int32, (ROWS8, w), 0)
    out = jnp.zeros((ROWS8, w), F32)
    for h in range(nheads):
        out = jnp.where(row // rows_per_head == h, tab_ref[(REL_BUCKETS - 1) * _TAB_COLS + col0 + h], out)
    return out


def _sample_kernel(*refs):
    np_ = PAGES_PER_STEP
    (pt_ref, tab_ref) = refs[0:2]
    (qn8_ref, qg8_ref, qr8_ref, qd8_ref, gates8_ref, tnsa_ref, tmla_ref, tdiff_ref, twin_ref,
     winbuf_ref) = refs[2:12]
    nsa_pages = refs[12:12 + np_]
    mla_pages = refs[12 + np_:12 + 2 * np_]
    diff_pages = refs[12 + 2 * np_:12 + 3 * np_]
    (wukt_ref, ngain_ref, wuv_ref, pem_ref, wc_ref, cgain_ref, pair_ref, dpar_ref) = refs[12 + 3 * np_:20 + 3 * np_]
    (onsa_ref, omla_ref, odiff_ref) = refs[20 + 3 * np_:23 + 3 * np_]
    (ut_sc, mm_sc, lm_sc, accm_sc, sd_sc, vd_sc, pooled_sc, ksvs_sc, selbuf_sc, bias_sc) = refs[23 + 3 * np_:]
    del pt_ref
    c = pl.program_id(1)
    nchunk = pl.num_programs(1)
    qpos = PAST_LEN

    @pl.when(c == 0)
    def _():
        mm_sc[...] = jnp.full(mm_sc.shape, NEG, F32)
        lm_sc[...] = jnp.zeros(lm_sc.shape, F32)
        accm_sc[...] = jnp.zeros(accm_sc.shape, F32)
        ut_sc[...] = jnp.dot(qg8_ref[0] * ngain_ref[...], wukt_ref[0:MLA_HEADS * MLA_NOPE, :],
                             precision=HIGHEST, preferred_element_type=F32)
        ksvs_sc[pl.ds(PAST_LEN, SEL_BLOCK), :] = jnp.zeros((SEL_BLOCK, LANE), F32)

    def mla_scores(xt):
        r = xt.shape[1]
        c_lat = xt[0:MLA_KV_LORA, :].astype(BF16)
        kkt = _mm(wukt_ref[...], c_lat)
        ssq = jnp.sum((kkt * kkt).reshape(ROWS8, MLA_NOPE, r), axis=1)
        num = _mm(ut_sc[...], c_lat)
        rope = _mm(qr8_ref[0], xt[MLA_KV_LORA:MLA_KV_LORA + MLA_ROPE, :])
        return (num * lax.rsqrt(ssq * (1.0 / MLA_NOPE) + RMS_EPS) + rope) * MLA_SCALE, c_lat

    def mla_accumulate(s, c_lat, keep):
        if keep is not None:
            s = jnp.where(keep, s, NEG)
        m_old = mm_sc[...]
        m_new = jnp.maximum(m_old, jnp.max(s, axis=-1, keepdims=True))
        alpha = jnp.exp(m_old - m_new)
        p = jnp.exp(s - m_new)
        if keep is not None:
            p = jnp.where(keep, p, 0.0)
        lm_sc[...] = alpha * lm_sc[...] + jnp.sum(p, axis=-1, keepdims=True)
        mm_sc[...] = m_new
        accm_sc[...] = alpha * accm_sc[...] + _mm_nt(p, c_lat)

    def diff_scores(x, bias):
        return _mm_nt(qd8_ref[0], x[:, 0:LANE]) * DIFF_SCALE + bias

    xn = jnp.concatenate([r[0, 0] for r in nsa_pages], axis=0)
    xmt = jnp.concatenate([r[0, 0] for r in mla_pages], axis=1)
    xd = jnp.concatenate([r[0, 0] for r in diff_pages], axis=0)
    k0 = c * CHUNK
    near = k0 + CHUNK - 1 > qpos - FAR_DIST
    dist_row = qpos - (k0 + lax.broadcasted_iota(jnp.int32, (1, CHUNK), 1))

    @pl.when(near)
    def _():
        bias_sc[...] = _bias_rows(dist_row, tab_ref, NSA_HEADS, DIFF_HEADS, 2)

    @pl.when(jnp.logical_not(near))
    def _():
        bias_sc[...] = _far_bias_rows(tab_ref, NSA_HEADS, DIFF_HEADS, 2, CHUNK)

    s_m, c_lat = mla_scores(xmt)
    lane_off = pl.multiple_of(k0, CHUNK)
    sd_sc[:, pl.ds(lane_off, CHUNK)] = diff_scores(xd, bias_sc[...])
    vd_sc[pl.ds(lane_off, CHUNK), :] = xd[:, LANE:2 * LANE].astype(BF16)
    mla_accumulate(s_m, c_lat, None)

    nblk = CHUNK // CMP_BLOCK
    pooled = jnp.sum(xn[:, 0:LANE].reshape(nblk, CMP_BLOCK, LANE), axis=1) * (1.0 / CMP_BLOCK)
    pooled_sc[pl.ds(pl.multiple_of(c * nblk, nblk), nblk), :] = pooled
    ksvs_sc[pl.ds(pl.multiple_of(k0, CHUNK), CHUNK), :] = xn[:, LANE:2 * LANE]

    @pl.when(c == nchunk - 1)
    def _():
        tail_keep = lax.broadcasted_iota(jnp.int32, (ROWS8, TAIL), 1) == 0
        s_t, c_t = mla_scores(tmla_ref[0])
        mla_accumulate(s_t, c_t, lax.broadcasted_iota(jnp.int32, (ROWS8, LANE), 1) == 0)
        lane_keep = lax.broadcasted_iota(jnp.int32, (ROWS8, LANE), 1) == 0
        xt_d = jnp.concatenate([tdiff_ref[0], jnp.zeros((LANE - TAIL, 2 * LANE), F32)], axis=0)
        s_td = diff_scores(xt_d, _bias_rows(jnp.zeros((1, LANE), jnp.int32), tab_ref, NSA_HEADS, DIFF_HEADS, 2))
        sd_sc[:, PAST_LEN:PAST_LEN + LANE] = jnp.where(lane_keep, s_td, NEG)
        vd_sc[PAST_LEN:PAST_LEN + LANE, :] = xt_d[:, LANE:2 * LANE].astype(BF16)
        ksvs_sc[pl.ds(PAST_LEN, TAIL), :] = tnsa_ref[0][:, LANE:2 * LANE]

        row8 = lax.broadcasted_iota(jnp.int32, (ROWS8, LANE), 0)
        lane8 = lax.broadcasted_iota(jnp.int32, (ROWS8, LANE), 1)
        lat = accm_sc[...] / lm_sc[...]
        o = jnp.dot(lat, wuv_ref[...], precision=HIGHEST, preferred_element_type=F32)
        r384 = lax.broadcasted_iota(jnp.int32, o.shape, 0)
        l384 = lax.broadcasted_iota(jnp.int32, o.shape, 1)
        omla_ref[0] = jnp.sum(jnp.where(l384 // MLA_V == r384, o, 0.0), axis=0, keepdims=True)

        s_all = sd_sc[...]
        e_d = jnp.exp(s_all - jnp.max(s_all, axis=-1, keepdims=True))
        p_d = e_d / jnp.sum(e_d, axis=-1, keepdims=True)
        a_d = p_d - dpar_ref[1:2, 0:1] * pltpu.roll(p_d, ROWS8 - 1, 0)
        a = jnp.dot(a_d.astype(BF16), vd_sc[...], preferred_element_type=F32)
        grp = row8 // 4
        a = jnp.where((lane8 >= grp * DIFF_V) & (lane8 < (grp + 1) * DIFF_V), a, 0.0)
        ms = jnp.sum(a * a, axis=-1, keepdims=True) * (1.0 / DIFF_V)
        odiff_ref[0] = a * lax.rsqrt(ms + RMS_EPS) * dpar_ref[0:1, :]

        qn8 = qn8_ref[0]
        nc = pooled_sc.shape[0]
        cm = _mm(pooled_sc[...] + pem_ref[...], wc_ref[...])
        lane_c = lax.broadcasted_iota(jnp.int32, cm.shape, 1)
        is_k = lane_c < HEAD_DIM
        msc = jnp.sum(jnp.where(is_k, cm * cm, 0.0), axis=-1, keepdims=True) * (1.0 / HEAD_DIM)
        cm = jnp.where(is_k, cm * lax.rsqrt(msc + RMS_EPS) * cgain_ref[...], cm)
        dist_c = qpos - ((lax.broadcasted_iota(jnp.int32, (1, nc), 1) + 1) * CMP_BLOCK - 1)
        s_c = _mm_nt(qn8, cm) * NSA_SCALE + _bias_rows(dist_c, tab_ref, 0, NSA_HEADS, 1)
        p_c = _masked_softmax_rows(s_c, dist_c >= 0)
        o_c = jnp.dot(p_c.astype(BF16), cm.astype(BF16), preferred_element_type=F32)
        rowc = lax.broadcasted_iota(jnp.int32, p_c.shape, 0)
        imp = jnp.sum(jnp.where(rowc < NSA_HEADS, p_c, 0.0), axis=0, keepdims=True)
        imp = jnp.dot(imp, pair_ref[...], precision=HIGHEST, preferred_element_type=F32)

        blk = lax.broadcasted_iota(jnp.int32, (1, N_SEL_PAD), 1)
        cur = qpos // SEL_BLOCK
        forced = (blk == 0) | (blk == cur) | (blk == cur - 1)
        score = jnp.where(blk * SEL_BLOCK <= qpos, jnp.where(forced, SEL_FORCED, imp), SEL_INVALID)
        lane_s = lax.broadcasted_iota(jnp.int32, (1, TOP_N * SEL_BLOCK), 1)
        pos = jnp.zeros((1, TOP_N * SEL_BLOCK), jnp.int32)
        for t in range(TOP_N):
            top = jnp.max(score)
            idx = jnp.min(jnp.where(score == top, blk, 1 << 20))
            score = jnp.where(blk == idx, SEL_TAKEN, score)
            selbuf_sc[t * SEL_BLOCK:(t + 1) * SEL_BLOCK, :] = ksvs_sc[pl.ds(pl.multiple_of(idx * SEL_BLOCK, SEL_BLOCK), SEL_BLOCK), :]
            pos = jnp.where(lane_s // SEL_BLOCK == t, idx * SEL_BLOCK + lane_s % SEL_BLOCK, pos)
        sb = selbuf_sc[...]
        dist_s = qpos - pos
        s_s = _mm_nt(qn8, sb) * NSA_SCALE + _bias_rows(dist_s, tab_ref, 0, NSA_HEADS, 1)
        p_s = _masked_softmax_rows(s_s, dist_s >= 0)
        o_s = jnp.dot(p_s.astype(BF16), sb.astype(BF16), preferred_element_type=F32)

        wb = winbuf_ref[0]
        nw = wb.shape[0]
        tw = twin_ref[0]
        dist_w = nw - lax.broadcasted_iota(jnp.int32, (1, nw), 1)
        keep_w = dist_w < WINDOW
        s_w = jnp.where(keep_w, _mm_nt(qn8, wb) * NSA_SCALE
                        + _bias_rows(dist_w, tab_ref, 0, NSA_HEADS, 1), NEG)
        s_n = jnp.where(tail_keep, _mm_nt(qn8, tw) * NSA_SCALE
                        + _bias_rows(jnp.zeros((1, TAIL), jnp.int32), tab_ref, 0, NSA_HEADS, 1), NEG)
        m_w = jnp.maximum(jnp.max(s_w, axis=-1, keepdims=True), jnp.max(s_n, axis=-1, keepdims=True))
        e_w = jnp.where(keep_w, jnp.exp(s_w - m_w), 0.0)
        e_n = jnp.where(tail_keep, jnp.exp(s_n - m_w), 0.0)
        z_w = jnp.sum(e_w, axis=-1, keepdims=True) + jnp.sum(e_n, axis=-1, keepdims=True)
        o_w = _mm(e_w / z_w, wb) + _mm(e_n / z_w, tw)

        g = gates8_ref[0]
        onsa_ref[0] = g[:, 0:1] * o_c + g[:, 1:2] * o_s + g[:, 2:3] * o_w


def _sample_weights(l, p):
    wukv = p['mla_w_ukv'][l]
    cols_k = np.concatenate([h * (MLA_NOPE + MLA_V) + np.arange(MLA_NOPE) for h in range(MLA_HEADS)])
    cols_v = np.concatenate([h * (MLA_NOPE + MLA_V) + MLA_NOPE + np.arange(MLA_V) for h in range(MLA_HEADS)])
    wukt = jnp.transpose(jnp.take(wukv, jnp.asarray(cols_k), axis=1))
    sw = {}
    sw['wukt'] = jnp.concatenate([wukt, jnp.zeros((ROWS8 * MLA_NOPE - wukt.shape[0], MLA_KV_LORA), F32)], axis=0)
    sw['ngain'] = jnp.tile(p['mla_nope_gain'][l][1], MLA_HEADS).reshape(1, -1)
    sw['wuv'] = jnp.take(wukv, jnp.asarray(cols_v), axis=1)
    return sw


def _sample_mixers(l, page_table, cache_nsa, cache_mla, cache_diff, win_buf, pr, lw, sw, table, dpar):
    nb = page_table.shape[0]
    npages = page_table.shape[1]
    assert npages * PAGE_SIZE == PAST_LEN and npages % PAGES_PER_STEP == 0
    nchunk = npages // PAGES_PER_STEP
    pad_rows = lambda a, n: jnp.concatenate([a, jnp.zeros((a.shape[0], n - a.shape[1], a.shape[2]), a.dtype)], axis=1)
    qn8 = pad_rows(pr['qnsa'].reshape(nb, NSA_HEADS, LANE), ROWS8)
    qm = pr['qmla'].reshape(nb, MLA_HEADS, LANE)
    eye = jnp.eye(ROWS8, MLA_HEADS, dtype=F32)
    qg8 = (eye[None, :, :, None] * qm[:, None, :, :MLA_NOPE]).reshape(nb, ROWS8, MLA_HEADS * MLA_NOPE)
    qr8 = pad_rows(qm[:, :, MLA_NOPE:MLA_NOPE + MLA_ROPE], ROWS8)
    qd8 = pr['qd'].reshape(nb, N_DMAPS, LANE)
    gates8 = pad_rows(jnp.concatenate([pr['gates'][:, :3 * NSA_HEADS].reshape(nb, NSA_HEADS, 3),
                                       jnp.zeros((nb, NSA_HEADS, LANE - 3), F32)], axis=2), ROWS8)
    mla_row = jnp.concatenate([pr['ckv'], pr['kr'][:, ROPE_LO:ROPE_LO + MLA_ROPE]], axis=1)
    tail = lambda a: pad_rows(a[:, None, :], TAIL)
    tnsa, tdiff, twin = tail(pr['nsarow']), tail(pr['diffrow']), tail(pr['winrow'])
    tmla = jnp.concatenate([mla_row[:, :, None], jnp.zeros((nb, mla_row.shape[1], LANE - 1), F32)], axis=2)
    cache_mla_t = jnp.swapaxes(cache_mla, 2, 3)
    pair = np.zeros((PAST_LEN // CMP_BLOCK, N_SEL_PAD), np.float32)
    ncb = PAST_LEN // CMP_BLOCK
    pair[np.arange(ncb), np.arange(ncb) // (SEL_BLOCK // CMP_BLOCK)] = 1.0

    seq = lambda a: pl.BlockSpec((1,) + tuple(a.shape[1:]), lambda b, c, pt: (b,) + (0,) * (a.ndim - 1))
    cst = lambda a: pl.BlockSpec(tuple(np.shape(a)), lambda b, c, pt: (0,) * np.ndim(a))

    def page_spec(cache, i):
        blk = (1, 1) + tuple(cache.shape[2:])
        return pl.BlockSpec(blk, lambda b, c, pt: (l, pt[b, c * PAGES_PER_STEP + i], 0, 0))

    seq_ins = [qn8, qg8, qr8, qd8, gates8, tnsa, tmla, tdiff, twin, win_buf]
    consts = [sw['wukt'], sw['ngain'], sw['wuv'], lw['pe_mean'], lw['wc'], lw['cgain'], pair, dpar]
    pages = [cache_nsa] * PAGES_PER_STEP + [cache_mla_t] * PAGES_PER_STEP + [cache_diff] * PAGES_PER_STEP
    in_specs = [pl.BlockSpec(memory_space=pltpu.SMEM)] + [seq(a) for a in seq_ins] \
        + [page_spec(cache_nsa, i) for i in range(PAGES_PER_STEP)] \
        + [page_spec(cache_mla_t, i) for i in range(PAGES_PER_STEP)] \
        + [page_spec(cache_diff, i) for i in range(PAGES_PER_STEP)] \
        + [cst(a) for a in consts]
    out_shape = [jax.ShapeDtypeStruct((nb, ROWS8, LANE), F32), jax.ShapeDtypeStruct((nb, 1, MLA_HEADS * MLA_V), F32),
                 jax.ShapeDtypeStruct((nb, ROWS8, LANE), F32)]
    out_specs = [pl.BlockSpec((1, ROWS8, LANE), lambda b, c, pt: (b, 0, 0)),
                 pl.BlockSpec((1, 1, MLA_HEADS * MLA_V), lambda b, c, pt: (b, 0, 0)),
                 pl.BlockSpec((1, ROWS8, LANE), lambda b, c, pt: (b, 0, 0))]
    r8 = (ROWS8, 1)
    scratch = [pltpu.VMEM((ROWS8, MLA_KV_LORA), F32),
               pltpu.VMEM(r8, F32), pltpu.VMEM(r8, F32), pltpu.VMEM((ROWS8, LANE), F32),
               pltpu.VMEM((ROWS8, PAST_LEN + LANE), F32), pltpu.VMEM((PAST_LEN + LANE, LANE), BF16),
               pltpu.VMEM((PAST_LEN // CMP_BLOCK, LANE), F32),
               pltpu.VMEM((PAST_LEN + SEL_BLOCK, LANE), F32),
               pltpu.VMEM((TOP_N * SEL_BLOCK, LANE), F32),
               pltpu.VMEM((ROWS8, CHUNK), F32)]
    grid_spec = pltpu.PrefetchScalarGridSpec(num_scalar_prefetch=1, grid=(nb, nchunk), in_specs=in_specs,
                                             out_specs=out_specs, scratch_shapes=scratch)
    onsa, omla, odiff = pl.pallas_call(
        _sample_kernel, grid_spec=grid_spec, out_shape=out_shape,
        compiler_params=pltpu.CompilerParams(dimension_semantics=("arbitrary", "arbitrary"),
                                             vmem_limit_bytes=VMEM_LIMIT),
        name="sample_paged_mixers")(page_table, table.reshape(-1), *seq_ins, *pages, *consts)
    nsa = onsa[:, :NSA_HEADS].reshape(nb, NSA_HEADS * LANE)
    dif = odiff[:, 0::2].reshape(nb, DIFF_HEADS * LANE)
    return nsa, omla[:, 0], dif, mla_row


PROMPT_TM = 256
PROMPT_FFN_TM = 512


def kernel(x_prompt, x_sample, cache_nsa, cache_mla, cache_diff, state_nsa_win, page_table, attn_norm, w_in, nsa_qk_gain, nsa_cmp_pe, nsa_cmp_w, mla_q_norm, mla_kv_norm, mla_w_uq, mla_w_ukv, mla_nope_gain, mla_rope_gain, diff_qk_gain, diff_lambda, diff_head_gain, w_out, rel_bias_table, ffn_norm, moe_w_group, moe_b_group, moe_w_router, moe_b_router, moe_w1, moe_w3, moe_w2):
    p = dict(attn_norm=attn_norm, w_in=w_in, nsa_qk_gain=nsa_qk_gain, nsa_cmp_pe=nsa_cmp_pe,
             nsa_cmp_w=nsa_cmp_w, mla_q_norm=mla_q_norm, mla_kv_norm=mla_kv_norm, mla_w_uq=mla_w_uq,
             mla_w_ukv=mla_w_ukv, mla_nope_gain=mla_nope_gain, mla_rope_gain=mla_rope_gain,
             diff_qk_gain=diff_qk_gain, diff_lambda=diff_lambda, diff_head_gain=diff_head_gain,
             w_out=w_out, ffn_norm=ffn_norm, moe_w_group=moe_w_group, moe_b_group=moe_b_group,
             moe_w_router=moe_w_router, moe_b_router=moe_b_router, moe_w1=moe_w1, moe_w3=moe_w3,
             moe_w2=moe_w2)
    assert x_prompt.shape[0] == 1 and x_sample.shape[1] == 1
    xp, xs = x_prompt[0], x_sample[:, 0]
    s, nb = xp.shape[0], xs.shape[0]
    pos_p = jnp.arange(s)
    pos_s = jnp.full((nb,), PAST_LEN, jnp.int32)
    table = rel_bias_table.astype(F32)
    wb = state_nsa_win.shape[2]
    outs = [[] for _ in range(8)]
    for l in range(w_in.shape[0]):
        lw, fw, sw, dpar = _layer_weights(l, p), _ffn_weights(l, p), _sample_weights(l, p), _diff_params(l, p)
        pp = _project(xp, pos_p, lw, tm=PROMPT_TM, fold_scale=True, qdtype=BF16)
        nsa = _nsa_attention(pp['qnsa'], pp['gates'], pp['nsarowb'], pp['winrowb'], pp['cmp'], table)
        mla = _mla_attention(pp['qmla'], pp['kmla'], pp['vmla'])
        dif = _diff_attention(pp['qd'], pp['diffrowb'], table, dpar)
        xp = _ffn(xp, nsa, mla, dif, fw, tm=PROMPT_FFN_TM)
        ps = _project(xs, pos_s, lw, tm=nb, fold_scale=False, qdtype=F32)
        win_buf = state_nsa_win[l]
        nsa_s, mla_s, dif_s, mla_row_s = _sample_mixers(l, page_table, cache_nsa, cache_mla, cache_diff,
                                                        win_buf, ps, lw, sw, table, dpar)
        xs = _ffn(xs, nsa_s, mla_s, dif_s, fw, tm=nb)
        mla_row_p = jnp.concatenate([pp['ckv'], pp['kr'][:, ROPE_LO:ROPE_LO + MLA_ROPE]], axis=1)
        new = (pp['nsarow'][None], ps['nsarow'][:, None], mla_row_p[None], mla_row_s[:, None],
               pp['diffrow'][None], ps['diffrow'][:, None], pp['winrow'][None, s - min(WINDOW, s):],
               jnp.concatenate([win_buf, ps['winrow'][:, None]], axis=1)[:, -wb:])
        for acc, a in zip(outs, new):
            acc.append(a)
    return (xp[None], xs[:, None]) + tuple(jnp.stack(a) for a in outs)
```

```python
import functools
import math

import numpy as np
import jax
import jax.numpy as jnp
from jax import lax
from jax.experimental import pallas as pl
from jax.experimental.pallas import tpu as pltpu

F32 = jnp.float32
BF16 = jnp.bfloat16
HIGHEST = lax.Precision.HIGHEST

D_MODEL = 1024
PAST_LEN = 16384
PAGE_SIZE = 128
HEAD_DIM = 64
NSA_HEADS = 6
CMP_BLOCK = 32
SEL_BLOCK = 64
TOP_N = 16
WINDOW = 512
MLA_HEADS = 6
MLA_Q_LORA = 192
MLA_KV_LORA = 128
MLA_NOPE = 64
MLA_ROPE = 32
MLA_V = 64
ROPE_THETA = 10000.0
DIFF_HEADS = 4
DIFF_KV_HEADS = 2
DIFF_QK = 32
DIFF_V = 64
REL_BUCKETS = 32
REL_MAX_DIST = 2048
N_GROUPS = 4
EXPERTS_PER_GROUP = 4
N_EXPERTS = N_GROUPS * EXPERTS_PER_GROUP
D_EXPERT = 256
RMS_EPS = 1e-6
NSA_SCALE = HEAD_DIM ** -0.5
MLA_SCALE = (MLA_NOPE + MLA_ROPE) ** -0.5
DIFF_SCALE = DIFF_QK ** -0.5
IN_SIZES = (384, 384, 18, 192, 128, 32, 256, 128, 128)
IN_WIDTH = sum(IN_SIZES)

LANE = 128
VMEM_LIMIT = 56 * 1024 * 1024

OFF_A, OFF_B, OFF_C, OFF_D, OFF_E, OFF_F, OFF_G, OFF_H = 0, 768, 1152, 1408, 1536, 2560, 2688, 2816
W_Y = 2944
N_GRP = 128
G_NSA, G_QN, G_KVN, G_DQK, G_RG, G_NG, G_ONE, G_ZERO = 0, 256, 448, 576, 704, 768, 896, 897
ROPE_LO = 64


def _layout_stage1():
    src = np.full((W_Y,), IN_WIDTH, np.int32)
    grp = np.full((W_Y,), -1, np.int32)
    gsrc = np.full((W_Y,), G_ZERO, np.int32)
    pas = np.zeros((W_Y,), np.float32)
    gsize = np.ones((N_GRP,), np.float32)
    o_q, o_kv, o_g, o_cq, o_ckv, o_kr, o_qd, o_kd, o_vd = np.cumsum((0,) + IN_SIZES[:-1])
    for h in range(NSA_HEADS):
        for d in range(HEAD_DIM):
            c = OFF_A + h * LANE + d
            src[c], grp[c], gsrc[c] = o_q + h * HEAD_DIM + d, h, G_NSA + d
        gsize[h] = HEAD_DIM
    for i in range(384):
        c = OFF_B + i
        src[c] = o_kv + i
        blk = i // HEAD_DIM
        if blk == 2:
            grp[c], gsrc[c] = 6, G_NSA + 2 * HEAD_DIM + i % HEAD_DIM
        elif blk == 4:
            grp[c], gsrc[c] = 7, G_NSA + 3 * HEAD_DIM + i % HEAD_DIM
        else:
            pas[c] = 1.0
    gsize[6] = gsize[7] = HEAD_DIM
    for i in range(MLA_Q_LORA):
        c = OFF_C + i
        src[c], grp[c], gsrc[c] = o_cq + i, 8, G_QN + i
    gsize[8] = MLA_Q_LORA
    for i in range(MLA_KV_LORA):
        c = OFF_D + i
        src[c], grp[c], gsrc[c] = o_ckv + i, 9, G_KVN + i
    gsize[9] = MLA_KV_LORA
    for g in range(2):
        for j in range(2):
            for m in range(2):
                r = g * 4 + j * 2 + m
                for d in range(DIFF_QK):
                    c = OFF_E + r * LANE + g * 64 + m * 32 + d
                    src[c] = o_qd + (g * 2 + j) * 64 + m * 32 + d
                    grp[c], gsrc[c] = 10 + r, G_DQK + m * 32 + d
                gsize[10 + r] = DIFF_QK
    for i in range(128):
        c = OFF_F + i
        gm = i // 32
        src[c], grp[c], gsrc[c] = o_kd + i, 18 + gm, G_DQK + (2 + gm % 2) * 32 + i % 32
        gsize[18 + gm] = DIFF_QK
    for i in range(128):
        c = OFF_G + i
        src[c], pas[c] = o_vd + i, 1.0
    for i in range(18):
        c = OFF_H + i
        src[c], pas[c] = o_g + i, 1.0
    for d in range(MLA_ROPE):
        c = OFF_H + ROPE_LO + d
        src[c], grp[c], gsrc[c] = o_kr + d, 22, G_RG + MLA_ROPE + d
    gsize[22] = MLA_ROPE
    return src, grp, gsrc, pas, gsize


def _layout_q_mla():
    w = MLA_HEADS * LANE
    src = np.full((w,), MLA_HEADS * (MLA_NOPE + MLA_ROPE), np.int32)
    grp = np.full((w,), -1, np.int32)
    gsrc = np.full((w,), G_ZERO, np.int32)
    gsize = np.ones((N_GRP,), np.float32)
    for h in range(MLA_HEADS):
        for t in range(MLA_NOPE + MLA_ROPE):
            c = h * LANE + t
            src[c] = h * (MLA_NOPE + MLA_ROPE) + t
            if t < MLA_NOPE:
                grp[c], gsrc[c] = 2 * h, G_NG + t
            else:
                grp[c], gsrc[c] = 2 * h + 1, G_RG + (t - MLA_NOPE)
        gsize[2 * h], gsize[2 * h + 1] = MLA_NOPE, MLA_ROPE
    return src, grp, gsrc, gsize


def _layout_kv_mla():
    wk = MLA_HEADS * LANE
    wv = MLA_HEADS * MLA_V
    src = np.full((wk + wv,), MLA_HEADS * (MLA_NOPE + MLA_V), np.int32)
    grp = np.full((wk,), -1, np.int32)
    gsrc = np.full((wk,), G_ZERO, np.int32)
    gsize = np.ones((N_GRP,), np.float32)
    for h in range(MLA_HEADS):
        for d in range(MLA_NOPE):
            c = h * LANE + d
            src[c], grp[c], gsrc[c] = h * (MLA_NOPE + MLA_V) + d, h, G_NG + MLA_NOPE + d
        gsize[h] = MLA_NOPE
        for d in range(MLA_V):
            src[wk + h * MLA_V + d] = h * (MLA_NOPE + MLA_V) + MLA_NOPE + d
    return src, grp, gsrc, gsize


def _group_mats(grp, gsize):
    w = grp.shape[0]
    g = np.zeros((w, N_GRP), np.float32)
    cols = np.nonzero(grp >= 0)[0]
    g[cols, grp[cols]] = 1.0
    return g, np.ascontiguousarray(g.T), (1.0 / gsize).reshape(1, N_GRP).astype(np.float32)


_S1 = _layout_stage1()
_S2 = _layout_q_mla()
_S3 = _layout_kv_mla()
_G1, _GT1, _INVN1 = _group_mats(_S1[1], _S1[4])
_G2, _GT2, _INVN2 = _group_mats(_S2[1], _S2[3])
_G3, _GT3, _INVN3 = _group_mats(_S3[1], _S3[3])


def _take_cols(w, src):
    wz = jnp.concatenate([w, jnp.zeros((w.shape[0], 1), w.dtype)], axis=1)
    return jnp.take(wz, jnp.asarray(src), axis=1)


def _flat_gains(nsa_gain, q_norm, kv_norm, dqk_gain, rope_gain, nope_gain):
    return jnp.concatenate([nsa_gain.reshape(-1), q_norm, kv_norm, dqk_gain.reshape(-1),
                            rope_gain.reshape(-1), nope_gain.reshape(-1),
                            jnp.ones((1,), F32), jnp.zeros((1,), F32)]).astype(F32)


def _rope_tables(pos):
    half = MLA_ROPE // 2
    inv = ROPE_THETA ** (-jnp.arange(half, dtype=jnp.float32) / half)
    ang = pos.astype(jnp.float32)[:, None] * inv[None, :]
    cos, sin = jnp.cos(ang), jnp.sin(ang)
    t = pos.shape[0]
    one = jnp.ones((t, ROPE_LO), F32)
    z16 = jnp.zeros((t, half), F32)
    z32 = jnp.zeros((t, LANE - ROPE_LO - MLA_ROPE), F32)
    z64 = jnp.zeros((t, ROPE_LO), F32)
    c_tab = jnp.concatenate([one, cos, cos, z32], axis=1)
    s1_tab = jnp.concatenate([z64, -sin, z16, z32], axis=1)
    s2_tab = jnp.concatenate([z64, z16, sin, z32], axis=1)
    return c_tab, s1_tab, s2_tab


def _mm(a, b):
    return jnp.dot(a.astype(BF16), b.astype(BF16), preferred_element_type=F32)


def _mm_nt(a, b):
    return lax.dot_general(a.astype(BF16), b.astype(BF16), (((1,), (1,)), ((), ())),
                           preferred_element_type=F32)


def _segnorm(y, g_ref, gt_ref, invn_ref, gain_ref, pass_row):
    ssq = jnp.dot(y * y, g_ref[...], precision=HIGHEST, preferred_element_type=F32)
    inv = lax.rsqrt(ssq * invn_ref[...] + RMS_EPS)
    fac = jnp.dot(inv, gt_ref[...], precision=HIGHEST, preferred_element_type=F32) * gain_ref[...]
    if pass_row is not None:
        fac = fac + pass_row
    return y * fac


def _rope_slab(x, c_tab, s1_tab, s2_tab):
    return (x * c_tab + pltpu.roll(x, LANE - MLA_ROPE // 2, 1) * s1_tab
            + pltpu.roll(x, MLA_ROPE // 2, 1) * s2_tab)


def _proj_kernel(x_ref, an_ref, w1_ref, g1_ref, gt1_ref, invn1_ref, gain1_ref, pass1_ref,
                 rc_ref, rs1_ref, rs2_ref, wuq_ref, g2_ref, gt2_ref, invn2_ref, gain2_ref,
                 wukv_ref, g3_ref, gt3_ref, invn3_ref, gain3_ref,
                 pool_ref, pem_ref, wc_ref, cgain_ref,
                 qnsa_ref, gates_ref, nsarow_ref, winrow_ref, qmla_ref, ckv_ref, kr_ref,
                 qd_ref, diffrow_ref, kmla_ref, vmla_ref, nsarowb_ref, winrowb_ref, diffrowb_ref,
                 cmp_ref, *, fold_scale):
    s_nsa, s_mla, s_diff = (NSA_SCALE, MLA_SCALE, DIFF_SCALE) if fold_scale else (1.0, 1.0, 1.0)
    x = x_ref[...]
    h = x * lax.rsqrt(jnp.mean(x * x, axis=-1, keepdims=True) + RMS_EPS) * an_ref[...]
    y = _mm(h, w1_ref[...])
    yn = _segnorm(y, g1_ref, gt1_ref, invn1_ref, gain1_ref, pass1_ref[...])
    rc, rs1, rs2 = rc_ref[...], rs1_ref[...], rs2_ref[...]

    qnsa_ref[...] = (yn[:, OFF_A:OFF_A + 768] * s_nsa).astype(qnsa_ref.dtype)
    nsa_row = yn[:, OFF_B:OFF_B + 256]
    nsarow_ref[...] = nsa_row
    nsarowb_ref[...] = nsa_row[:, LANE:2 * LANE].astype(BF16)
    win_row = yn[:, OFF_B + 256:OFF_B + 384]
    winrow_ref[...] = win_row
    winrowb_ref[...] = win_row.astype(BF16)
    ckv = yn[:, OFF_D:OFF_D + 128]
    ckv_ref[...] = ckv
    qd_ref[...] = (yn[:, OFF_E:OFF_E + 1024] * s_diff).astype(qd_ref.dtype)
    diff_row = yn[:, OFF_F:OFF_F + 256]
    diffrow_ref[...] = diff_row
    diffrowb_ref[...] = diff_row.astype(BF16)
    y_h = y[:, OFF_H:OFF_H + 128]
    gates_ref[...] = jax.nn.sigmoid(y_h)
    kr = _rope_slab(yn[:, OFF_H:OFF_H + 128], rc, rs1, rs2)
    lane = lax.broadcasted_iota(jnp.int32, kr.shape, 1)
    kr = jnp.where((lane >= ROPE_LO) & (lane < ROPE_LO + MLA_ROPE), kr, 0.0)
    kr_ref[...] = kr

    qf = _mm(yn[:, OFF_C:OFF_C + 256], wuq_ref[...])
    qf = _segnorm(qf, g2_ref, gt2_ref, invn2_ref, gain2_ref, None)
    for hh in range(MLA_HEADS):
        sl = slice(hh * LANE, (hh + 1) * LANE)
        qmla_ref[:, sl] = (_rope_slab(qf[:, sl], rc, rs1, rs2) * s_mla).astype(qmla_ref.dtype)

    kv = _mm(ckv, wukv_ref[...])
    kn = _segnorm(kv[:, :768], g3_ref, gt3_ref, invn3_ref, gain3_ref, None)
    for hh in range(MLA_HEADS):
        sl = slice(hh * LANE, (hh + 1) * LANE)
        kmla_ref[:, sl] = (kn[:, sl] + kr).astype(kmla_ref.dtype)
    vmla_ref[...] = kv[:, 768:].astype(vmla_ref.dtype)

    pooled = jnp.dot(pool_ref[...], nsa_row[:, 0:128], precision=HIGHEST,
                     preferred_element_type=F32) + pem_ref[...]
    cm = jnp.dot(pooled, wc_ref[...], precision=HIGHEST, preferred_element_type=F32)
    lane_c = lax.broadcasted_iota(jnp.int32, cm.shape, 1)
    is_k = lane_c < HEAD_DIM
    ms = jnp.sum(jnp.where(is_k, cm * cm, 0.0), axis=-1, keepdims=True) * (1.0 / HEAD_DIM)
    cmp_ref[...] = jnp.where(is_k, cm * lax.rsqrt(ms + RMS_EPS) * cgain_ref[...], cm)


def _const_spec(shape):
    return pl.BlockSpec(shape, lambda i: (0,) * len(shape))


def _project(x, pos, lw, *, tm, fold_scale, qdtype):
    t = x.shape[0]
    assert t % tm == 0 and tm % CMP_BLOCK == 0
    nt = t // tm
    rc, rs1, rs2 = _rope_tables(pos)
    wdt = BF16
    pool = np.zeros((tm // CMP_BLOCK, tm), np.float32)
    for r in range(tm // CMP_BLOCK):
        pool[r, r * CMP_BLOCK:(r + 1) * CMP_BLOCK] = 1.0 / CMP_BLOCK
    ins = [x, lw['attn_norm'], lw['w1'].astype(wdt), _G1, _GT1, _INVN1, lw['gain1'], lw['pass1'],
           rc, rs1, rs2, lw['wuq'].astype(wdt), _G2, _GT2, _INVN2, lw['gain2'],
           lw['wukv'].astype(wdt), _G3, _GT3, _INVN3, lw['gain3'],
           pool, lw['pe_mean'], lw['wc'], lw['cgain']]
    tok = lambda w: pl.BlockSpec((tm, w), lambda i: (i, 0))
    in_specs = [tok(D_MODEL)] + [_const_spec(tuple(np.shape(a))) for a in ins[1:8]] \
        + [tok(LANE)] * 3 + [_const_spec(tuple(np.shape(a))) for a in ins[11:]]
    widths = dict(qnsa=(768, qdtype), gates=(128, F32), nsarow=(256, F32), winrow=(128, F32),
                  qmla=(768, qdtype), ckv=(128, F32), kr=(128, F32), qd=(1024, qdtype),
                  diffrow=(256, F32), kmla=(768, qdtype), vmla=(384, qdtype),
                  nsarowb=(128, BF16), winrowb=(128, BF16), diffrowb=(256, BF16))
    out_shape = [jax.ShapeDtypeStruct((t, w), dt) for (w, dt) in widths.values()]
    out_specs = [tok(w) for (w, _) in widths.values()]
    out_shape.append(jax.ShapeDtypeStruct((t // CMP_BLOCK, LANE), F32))
    out_specs.append(pl.BlockSpec((tm // CMP_BLOCK, LANE), lambda i: (i, 0)))
    outs = pl.pallas_call(
        functools.partial(_proj_kernel, fold_scale=fold_scale),
        grid=(nt,), in_specs=in_specs, out_specs=out_specs, out_shape=out_shape,
        compiler_params=pltpu.CompilerParams(dimension_semantics=("arbitrary",),
                                             vmem_limit_bytes=VMEM_LIMIT),
        name="token_projection",
    )(*ins)
    res = dict(zip(list(widths.keys()) + ['cmp'], outs))
    return res


def _layer_weights(l, p):
    flat = _flat_gains(p['nsa_qk_gain'][l], p['mla_q_norm'][l], p['mla_kv_norm'][l],
                       p['diff_qk_gain'][l], p['mla_rope_gain'][l], p['mla_nope_gain'][l])
    lw = {}
    lw['attn_norm'] = p['attn_norm'][l].reshape(1, D_MODEL)
    lw['w1'] = _take_cols(p['w_in'][l], _S1[0])
    lw['gain1'] = jnp.take(flat, jnp.asarray(_S1[2])).reshape(1, W_Y)
    lw['pass1'] = jnp.asarray(_S1[3]).reshape(1, W_Y)
    wuq = _take_cols(p['mla_w_uq'][l], _S2[0])
    lw['wuq'] = jnp.concatenate([wuq, jnp.zeros((256 - MLA_Q_LORA, wuq.shape[1]), F32)], axis=0)
    lw['gain2'] = jnp.take(flat, jnp.asarray(_S2[2])).reshape(1, -1)
    lw['wukv'] = _take_cols(p['mla_w_ukv'][l], _S3[0])
    lw['gain3'] = jnp.take(flat, jnp.asarray(_S3[2])).reshape(1, -1)
    pe = p['nsa_cmp_pe'][l]
    lw['pe_mean'] = jnp.concatenate([jnp.mean(pe[0], axis=0), jnp.mean(pe[1], axis=0)]).reshape(1, LANE)
    wk, wv = p['nsa_cmp_w'][l][0], p['nsa_cmp_w'][l][1]
    z = jnp.zeros((HEAD_DIM, HEAD_DIM), F32)
    lw['wc'] = jnp.concatenate([jnp.concatenate([wk, z], axis=1), jnp.concatenate([z, wv], axis=1)], axis=0)
    lw['cgain'] = jnp.concatenate([p['nsa_qk_gain'][l][1], jnp.ones((HEAD_DIM,), F32)]).reshape(1, LANE)
    return lw


def _bucket_np(n):
    n = np.asarray(n)
    nf = np.maximum(n, 1).astype(np.float64)
    large = 16 + (np.log(nf / 16.0) / math.log(REL_MAX_DIST / 16.0) * 16.0).astype(np.int64)
    return np.where(n < 16, n, np.minimum(large, REL_BUCKETS - 1)).astype(np.int32)


_BUCKET_LB = [int(np.nonzero(_bucket_np(np.arange(4096)) >= b)[0][0]) for b in range(REL_BUCKETS)]
FAR_DIST = _BUCKET_LB[-1]
NEG = -1e30

TQ = 256
TK = 512
BAND_ROWS = 128
BAND_C = 2432
BAND_W = BAND_C + TK
assert BAND_C - (TK - 1) >= FAR_DIST and BAND_C % LANE == 0 and TQ % BAND_ROWS == 0

_BAND_BID = _bucket_np(np.maximum(
    BAND_C + np.arange(BAND_ROWS)[:, None] - np.arange(BAND_W)[None, :], 0))


def _bias_from_dist(dist, tab_ref, col):
    b = jnp.full(dist.shape, tab_ref[0 * _TAB_COLS + col], F32)
    for k in range(1, REL_BUCKETS):
        b = jnp.where(dist >= _BUCKET_LB[k], tab_ref[k * _TAB_COLS + col], b)
    return b


_TAB_COLS = NSA_HEADS + DIFF_HEADS


def _build_band(band_ref, bid_ref, tab_ref, col0, nheads):
    def per_head(h, carry):
        band_ref[h] = jnp.full(bid_ref.shape, tab_ref[(REL_BUCKETS - 1) * _TAB_COLS + col0 + h], F32)

        def per_bucket(k, c):
            band_ref[h] = jnp.where(bid_ref[...] == k, tab_ref[k * _TAB_COLS + col0 + h], band_ref[h])
            return c

        lax.fori_loop(0, REL_BUCKETS - 1, per_bucket, 0)
        return carry

    lax.fori_loop(0, nheads, per_head, 0)


def _band_tile(band_ref, h, delta):
    parts = []
    for rb in range(TQ // BAND_ROWS):
        c0 = BAND_C - jnp.minimum(delta + rb * BAND_ROWS, BAND_C)
        parts.append(band_ref[h, :, pl.ds(pl.multiple_of(c0, LANE), TK)])
    return jnp.concatenate(parts, axis=0)


def _causal_pairs(s):
    qi, ki = [], []
    for a in range(s // TQ):
        for b in range((a * TQ + TQ - 1) // TK + 1):
            qi.append(a)
            ki.append(b)
    return np.asarray(qi, np.int32), np.asarray(ki, np.int32)


def _softmax_step(s, m_ref, l_ref, acc_ref, idx, v, keep=None):
    m_old = m_ref[idx]
    m_new = jnp.maximum(m_old, jnp.max(s, axis=-1, keepdims=True))
    alpha = jnp.exp(m_old - m_new)
    p = jnp.exp(s - m_new)
    if keep is not None:
        p = jnp.where(keep, p, 0.0)
    l_ref[idx] = alpha * l_ref[idx] + jnp.sum(p, axis=-1, keepdims=True)
    m_ref[idx] = m_new
    acc_ref[idx] = alpha * acc_ref[idx] + jnp.dot(p.astype(BF16), v, preferred_element_type=F32)


def _init_softmax(m_ref, l_ref, acc_ref):
    m_ref[...] = jnp.full(m_ref.shape, NEG, F32)
    l_ref[...] = jnp.zeros(l_ref.shape, F32)
    acc_ref[...] = jnp.zeros(acc_ref.shape, F32)


def _tile_positions(q0, k0):
    rows = q0 + lax.broadcasted_iota(jnp.int32, (TQ, TK), 0)
    cols = k0 + lax.broadcasted_iota(jnp.int32, (TQ, TK), 1)
    return rows, cols


def _mla_attn_kernel(qt_ref, kt_ref, q_ref, k_ref, v_ref, o_ref, m_sc, l_sc, acc_sc):
    step = pl.program_id(0)
    qi, ki = qt_ref[step], kt_ref[step]
    q0, k0 = qi * TQ, ki * TK

    @pl.when(ki == 0)
    def _():
        _init_softmax(m_sc, l_sc, acc_sc)

    def update(masked):
        if masked:
            rows, cols = _tile_positions(q0, k0)
            keep = cols <= rows
        for h in range(MLA_HEADS):
            sl = slice(h * LANE, (h + 1) * LANE)
            s = _mm_nt(q_ref[:, sl], k_ref[:, sl])
            if masked:
                s = jnp.where(keep, s, NEG)
            pr = slice((h // 2) * LANE, (h // 2 + 1) * LANE)
            _softmax_step(s, m_sc, l_sc, acc_sc, h, v_ref[:, pr])

    on_diag = k0 + TK - 1 > q0
    pl.when(on_diag)(lambda: update(True))
    pl.when(jnp.logical_not(on_diag))(lambda: update(False))

    @pl.when(ki == (q0 + TQ - 1) // TK)
    def _():
        lane = lax.broadcasted_iota(jnp.int32, (TQ, LANE), 1)
        for pi in range(MLA_HEADS // 2):
            lo = acc_sc[2 * pi] / l_sc[2 * pi]
            hi = acc_sc[2 * pi + 1] / l_sc[2 * pi + 1]
            o_ref[:, pi * LANE:(pi + 1) * LANE] = jnp.where(lane < MLA_V, lo, hi)


def _mla_attention(q, k, v):
    s = q.shape[0]
    qt, kt = _causal_pairs(s)
    grid_spec = pltpu.PrefetchScalarGridSpec(
        num_scalar_prefetch=2, grid=(qt.shape[0],),
        in_specs=[pl.BlockSpec((TQ, 768), lambda i, a, b: (a[i], 0)),
                  pl.BlockSpec((TK, 768), lambda i, a, b: (b[i], 0)),
                  pl.BlockSpec((TK, 384), lambda i, a, b: (b[i], 0))],
        out_specs=pl.BlockSpec((TQ, 384), lambda i, a, b: (a[i], 0)),
        scratch_shapes=[pltpu.VMEM((MLA_HEADS, TQ, 1), F32), pltpu.VMEM((MLA_HEADS, TQ, 1), F32),
                        pltpu.VMEM((MLA_HEADS, TQ, LANE), F32)])
    return pl.pallas_call(
        _mla_attn_kernel, grid_spec=grid_spec, out_shape=jax.ShapeDtypeStruct((s, 384), F32),
        compiler_params=pltpu.CompilerParams(dimension_semantics=("arbitrary",),
                                             vmem_limit_bytes=VMEM_LIMIT),
        name="mla_prompt_attention")(qt, kt, q, k, v)


N_DMAPS = 2 * DIFF_HEADS


def _diff_attn_kernel(qt_ref, kt_ref, tab_ref, bid_ref, q_ref, kv_ref, par_ref, o_ref,
                      band_sc, m_sc, l_sc, acc_sc):
    step = pl.program_id(0)
    qi, ki = qt_ref[step], kt_ref[step]
    q0, k0 = qi * TQ, ki * TK

    @pl.when(step == 0)
    def _():
        _build_band(band_sc, bid_ref, tab_ref, NSA_HEADS, DIFF_HEADS)

    @pl.when(ki == 0)
    def _():
        _init_softmax(m_sc, l_sc, acc_sc)

    def update(masked):
        if masked:
            rows, cols = _tile_positions(q0, k0)
            keep = cols <= rows
        k = kv_ref[:, 0:LANE]
        v = kv_ref[:, LANE:2 * LANE]
        for h in range(DIFF_HEADS):
            bias = _band_tile(band_sc, h, q0 - k0)
            for m in range(2):
                r = 2 * h + m
                s = _mm_nt(q_ref[:, r * LANE:(r + 1) * LANE], k) + bias
                if masked:
                    s = jnp.where(keep, s, NEG)
                _softmax_step(s, m_sc, l_sc, acc_sc, r, v)

    on_diag = k0 + TK - 1 > q0
    pl.when(on_diag)(lambda: update(True))
    pl.when(jnp.logical_not(on_diag))(lambda: update(False))

    @pl.when(ki == (q0 + TQ - 1) // TK)
    def _():
        lane = lax.broadcasted_iota(jnp.int32, (TQ, LANE), 1)
        gain = par_ref[0:1, :]
        lam = par_ref[1:2, :]
        for h in range(DIFF_HEADS):
            g = h // 2
            o = acc_sc[2 * h] / l_sc[2 * h] - lam * (acc_sc[2 * h + 1] / l_sc[2 * h + 1])
            o = jnp.where((lane >= g * DIFF_V) & (lane < (g + 1) * DIFF_V), o, 0.0)
            ms = jnp.sum(o * o, axis=-1, keepdims=True) * (1.0 / DIFF_V)
            o_ref[:, h * LANE:(h + 1) * LANE] = o * lax.rsqrt(ms + RMS_EPS) * gain


def _diff_params(l, p):
    lam_init = 0.8 - 0.6 * math.exp(-0.3 * l)
    lv = p['diff_lambda'][l].astype(F32)
    lam = jnp.exp(jnp.sum(lv[0] * lv[1])) - jnp.exp(jnp.sum(lv[2] * lv[3])) + lam_init
    gain = jnp.tile(p['diff_head_gain'][l].astype(F32), 2) * (1.0 - lam_init)
    return jnp.concatenate([gain.reshape(1, LANE), jnp.full((1, LANE), lam, F32),
                            jnp.zeros((6, LANE), F32)], axis=0)


def _diff_attention(q, kv, table, dpar):
    s = q.shape[0]
    qt, kt = _causal_pairs(s)
    smem = pl.BlockSpec(memory_space=pltpu.SMEM)
    grid_spec = pltpu.PrefetchScalarGridSpec(
        num_scalar_prefetch=2, grid=(qt.shape[0],),
        in_specs=[smem,
                  pl.BlockSpec((BAND_ROWS, BAND_W), lambda i, a, b: (0, 0)),
                  pl.BlockSpec((TQ, 1024), lambda i, a, b: (a[i], 0)),
                  pl.BlockSpec((TK, 256), lambda i, a, b: (b[i], 0)),
                  pl.BlockSpec((8, LANE), lambda i, a, b: (0, 0))],
        out_specs=pl.BlockSpec((TQ, 512), lambda i, a, b: (a[i], 0)),
        scratch_shapes=[pltpu.VMEM((DIFF_HEADS, BAND_ROWS, BAND_W), F32),
                        pltpu.VMEM((N_DMAPS, TQ, 1), F32), pltpu.VMEM((N_DMAPS, TQ, 1), F32),
                        pltpu.VMEM((N_DMAPS, TQ, LANE), F32)])
    return pl.pallas_call(
        _diff_attn_kernel, grid_spec=grid_spec, out_shape=jax.ShapeDtypeStruct((s, 512), F32),
        compiler_params=pltpu.CompilerParams(dimension_semantics=("arbitrary",),
                                             vmem_limit_bytes=VMEM_LIMIT),
        name="diff_prompt_attention")(qt, kt, table.reshape(-1), _BAND_BID, q, kv, dpar)


SEL_FORCED = 1e30
SEL_INVALID = -1e30
SEL_TAKEN = -2e30
SEL_SHIFT = SEL_BLOCK.bit_length() - 1
assert 1 << SEL_SHIFT == SEL_BLOCK


def _select_blocks(score, blk):
    sel = jnp.zeros(score.shape, jnp.bool_)
    for _ in range(TOP_N):
        m = jnp.max(score, axis=-1, keepdims=True)
        idx = jnp.min(jnp.where(score == m, blk, 1 << 20), axis=-1, keepdims=True)
        hit = blk == idx
        sel = sel | hit
        score = jnp.where(hit, SEL_TAKEN, score)
    return sel


def _masked_softmax_rows(s, keep):
    s = jnp.where(keep, s, NEG)
    m = jnp.max(s, axis=-1, keepdims=True)
    e = jnp.where(keep, jnp.exp(s - m), 0.0)
    z = jnp.sum(e, axis=-1, keepdims=True)
    return e / jnp.where(z > 0.0, z, 1.0)


def _nsa_attn_kernel(qt_ref, kt_ref, tab_ref, lb_ref, bid_ref, q_ref, gates_ref, ksel_ref, kwin_ref, cmp_ref,
                     pair_ref, o_ref, band_sc, sel_sc, oc_sc, m_sc, l_sc, acc_sc, mw_sc, lw_sc, accw_sc,
                     cbias_sc):
    step = pl.program_id(0)
    qi, ki = qt_ref[step], kt_ref[step]
    q0, k0 = qi * TQ, ki * TK
    nc = cmp_ref.shape[0]
    ns = pair_ref.shape[1]

    @pl.when(step == 0)
    def _():
        _build_band(band_sc, bid_ref, tab_ref, 0, NSA_HEADS)

    @pl.when(ki == 0)
    def _():
        _init_softmax(m_sc, l_sc, acc_sc)
        _init_softmax(mw_sc, lw_sc, accw_sc)
        cm = cmp_ref[...].astype(BF16)
        qpos = q0 + lax.broadcasted_iota(jnp.int32, (TQ, nc), 0)
        cend = (lax.broadcasted_iota(jnp.int32, (TQ, nc), 1) + 1) * CMP_BLOCK - 1
        dist = qpos - cend
        keep = dist >= 0
        for h in range(NSA_HEADS):
            cbias_sc[h] = jnp.full((TQ, nc), tab_ref[h], F32)

        def per_bucket(k, c):
            far = dist >= lb_ref[k]
            for h in range(NSA_HEADS):
                cbias_sc[h] = jnp.where(far, tab_ref[k * _TAB_COLS + h], cbias_sc[h])
            return c

        lax.fori_loop(1, REL_BUCKETS, per_bucket, 0)
        imp = jnp.zeros((TQ, nc), F32)
        for h in range(NSA_HEADS):
            s = _mm_nt(q_ref[:, h * LANE:(h + 1) * LANE], cm)
            p = _masked_softmax_rows(s + cbias_sc[h], keep)
            oc_sc[h] = jnp.dot(p.astype(BF16), cm, preferred_element_type=F32)
            imp = imp + p
        imp = jnp.dot(imp, pair_ref[...], precision=HIGHEST, preferred_element_type=F32)
        blk = lax.broadcasted_iota(jnp.int32, (TQ, ns), 1)
        qp = q0 + lax.broadcasted_iota(jnp.int32, (TQ, ns), 0)
        cur = qp // SEL_BLOCK
        forced = (blk == 0) | (blk == cur) | (blk == cur - 1)
        valid = blk * SEL_BLOCK <= qp
        score = jnp.where(valid, jnp.where(forced, SEL_FORCED, imp), SEL_INVALID)
        sel_sc[...] = _select_blocks(score, blk).astype(F32)

    rows, cols = _tile_positions(q0, k0)
    dist = rows - cols
    eb = lax.broadcasted_iota(jnp.int32, (ns, TK), 0)
    ek = k0 + lax.broadcasted_iota(jnp.int32, (ns, TK), 1)
    expand = jnp.where(eb == jnp.right_shift(ek, SEL_SHIFT), 1.0, 0.0).astype(BF16)
    chosen = jnp.dot(sel_sc[...].astype(BF16), expand, preferred_element_type=F32)
    keep_s = (chosen > 0.5) & (dist >= 0)
    in_win = k0 + TK - 1 >= q0 - (WINDOW - 1)
    kv = ksel_ref[...]
    for h in range(NSA_HEADS):
        s = _mm_nt(q_ref[:, h * LANE:(h + 1) * LANE], kv) + _band_tile(band_sc, h, q0 - k0)
        _softmax_step(jnp.where(keep_s, s, NEG), m_sc, l_sc, acc_sc, h, kv)

    @pl.when(in_win)
    def _():
        keep_w = (dist >= 0) & (dist < WINDOW)
        kw = kwin_ref[...]
        for h in range(NSA_HEADS):
            s = _mm_nt(q_ref[:, h * LANE:(h + 1) * LANE], kw) + _band_tile(band_sc, h, q0 - k0)
            _softmax_step(jnp.where(keep_w, s, NEG), mw_sc, lw_sc, accw_sc, h, kw, keep=keep_w)

    @pl.when(ki == (q0 + TQ - 1) // TK)
    def _():
        lane = lax.broadcasted_iota(jnp.int32, (TQ, LANE), 1)
        for h in range(NSA_HEADS):
            o = (gates_ref[:, 3 * h:3 * h + 1] * oc_sc[h]
                 + gates_ref[:, 3 * h + 1:3 * h + 2] * (acc_sc[h] / l_sc[h])
                 + gates_ref[:, 3 * h + 2:3 * h + 3] * (accw_sc[h] / lw_sc[h]))
            o_ref[:, h * LANE:(h + 1) * LANE] = jnp.where(lane >= HEAD_DIM, o, 0.0)


def _first_window_tile(q0):
    return jnp.maximum(q0 - (WINDOW - 1), 0) // TK


def _nsa_attention(q, gates, nsarow_b, winrow_b, cmp, table):
    s = q.shape[0]
    nc, ns = s // CMP_BLOCK, s // SEL_BLOCK
    qt, kt = _causal_pairs(s)
    pair = np.zeros((nc, ns), np.float32)
    pair[np.arange(nc), np.arange(nc) // (SEL_BLOCK // CMP_BLOCK)] = 1.0
    smem = pl.BlockSpec(memory_space=pltpu.SMEM)
    hq = (NSA_HEADS, TQ)
    grid_spec = pltpu.PrefetchScalarGridSpec(
        num_scalar_prefetch=2, grid=(qt.shape[0],),
        in_specs=[smem, smem,
                  pl.BlockSpec((BAND_ROWS, BAND_W), lambda i, a, b: (0, 0)),
                  pl.BlockSpec((TQ, 768), lambda i, a, b: (a[i], 0)),
                  pl.BlockSpec((TQ, LANE), lambda i, a, b: (a[i], 0)),
                  pl.BlockSpec((TK, LANE), lambda i, a, b: (b[i], 0)),
                  pl.BlockSpec((TK, LANE), lambda i, a, b: (jnp.maximum(b[i], _first_window_tile(a[i] * TQ)), 0)),
                  pl.BlockSpec((nc, LANE), lambda i, a, b: (0, 0)),
                  pl.BlockSpec((nc, ns), lambda i, a, b: (0, 0))],
        out_specs=pl.BlockSpec((TQ, 768), lambda i, a, b: (a[i], 0)),
        scratch_shapes=[pltpu.VMEM((NSA_HEADS, BAND_ROWS, BAND_W), F32),
                        pltpu.VMEM((TQ, ns), F32), pltpu.VMEM(hq + (LANE,), F32),
                        pltpu.VMEM(hq + (1,), F32), pltpu.VMEM(hq + (1,), F32), pltpu.VMEM(hq + (LANE,), F32),
                        pltpu.VMEM(hq + (1,), F32), pltpu.VMEM(hq + (1,), F32), pltpu.VMEM(hq + (LANE,), F32),
                        pltpu.VMEM(hq + (nc,), F32)])
    return pl.pallas_call(
        _nsa_attn_kernel, grid_spec=grid_spec, out_shape=jax.ShapeDtypeStruct((s, 768), F32),
        compiler_params=pltpu.CompilerParams(dimension_semantics=("arbitrary",),
                                             vmem_limit_bytes=VMEM_LIMIT),
        name="nsa_prompt_attention")(qt, kt, table.reshape(-1), np.asarray(_BUCKET_LB, np.int32), _BAND_BID,
                                     q, gates, nsarow_b, winrow_b, cmp, pair)


ROUTE_E0 = N_GROUPS


def _ffn_kernel(x_ref, nsa_ref, mla_ref, dif_ref, won_ref, wom_ref, wod_ref, fn_ref, wr_ref, br_ref,
                w1_ref, w3_ref, w2_ref, y_ref, xn_sc, hb_sc, comb_sc, acc_sc):
    e = pl.program_id(1)
    tm = x_ref.shape[0]
    lane = lax.broadcasted_iota(jnp.int32, (tm, LANE), 1)

    @pl.when(e == 0)
    def _():
        attn = (_mm(nsa_ref[...], won_ref[...]) + _mm(mla_ref[...], wom_ref[...])
                + _mm(dif_ref[...], wod_ref[...]))
        xn = x_ref[...] + attn
        xn_sc[...] = xn
        h = xn * lax.rsqrt(jnp.mean(xn * xn, axis=-1, keepdims=True) + RMS_EPS) * fn_ref[...]
        hb = h.astype(BF16)
        hb_sc[...] = hb
        logits = jnp.dot(hb, wr_ref[...], preferred_element_type=F32) + br_ref[...]
        big = 1 << 20
        is_g = lane < N_GROUPS
        gl = jnp.where(is_g, logits, NEG)
        gmax = jnp.max(gl, axis=-1, keepdims=True)
        gidx = jnp.min(jnp.where(gl == gmax, lane, big), axis=-1, keepdims=True)
        p_g = 1.0 / jnp.sum(jnp.where(is_g, jnp.exp(gl - gmax), 0.0), axis=-1, keepdims=True)
        in_grp = (lane >= ROUTE_E0) & (lane < ROUTE_E0 + N_EXPERTS) \
            & ((lane - ROUTE_E0) // EXPERTS_PER_GROUP == gidx)
        e1 = jnp.where(in_grp, logits, NEG)
        v1 = jnp.max(e1, axis=-1, keepdims=True)
        i1 = jnp.min(jnp.where(e1 == v1, lane, big), axis=-1, keepdims=True)
        e2 = jnp.where(lane == i1, NEG, e1)
        v2 = jnp.max(e2, axis=-1, keepdims=True)
        i2 = jnp.min(jnp.where(e2 == v2, lane, big), axis=-1, keepdims=True)
        t = jnp.exp(v2 - v1)
        w_top = p_g / (1.0 + t)
        comb_sc[...] = jnp.where(lane == i1, w_top, 0.0) + jnp.where(lane == i2, w_top * t, 0.0)
        acc_sc[...] = jnp.zeros(acc_sc.shape, F32)

    hb = hb_sc[...]
    a = jnp.dot(hb, w1_ref[0], preferred_element_type=F32)
    b = jnp.dot(hb, w3_ref[0], preferred_element_type=F32)
    cw = jnp.sum(jnp.where(lane == ROUTE_E0 + e, comb_sc[...], 0.0), axis=-1, keepdims=True)
    act = a * jax.nn.sigmoid(a) * b * cw
    acc_sc[...] += jnp.dot(act.astype(BF16), w2_ref[0], preferred_element_type=F32)

    @pl.when(e == N_EXPERTS - 1)
    def _():
        y_ref[...] = xn_sc[...] + acc_sc[...]


def _ffn_weights(l, p):
    wo = p['w_out'][l]
    z = wo.shape[1]
    rows_n = np.full((NSA_HEADS * LANE,), wo.shape[0], np.int32)
    for h in range(NSA_HEADS):
        rows_n[h * LANE + HEAD_DIM + np.arange(HEAD_DIM)] = h * HEAD_DIM + np.arange(HEAD_DIM)
    rows_d = np.full((DIFF_HEADS * LANE,), wo.shape[0], np.int32)
    for h in range(DIFF_HEADS):
        rows_d[h * LANE + (h // 2) * DIFF_V + np.arange(DIFF_V)] = 768 + h * DIFF_V + np.arange(DIFF_V)
    woz = jnp.concatenate([wo, jnp.zeros((1, z), wo.dtype)], axis=0)
    fw = {}
    fw['won'] = jnp.take(woz, jnp.asarray(rows_n), axis=0)
    fw['wom'] = wo[384:768]
    fw['wod'] = jnp.take(woz, jnp.asarray(rows_d), axis=0)
    fw['fnorm'] = p['ffn_norm'][l].reshape(1, D_MODEL)
    wrt = jnp.transpose(p['moe_w_router'][l], (1, 0, 2)).reshape(D_MODEL, N_EXPERTS)
    pad = jnp.zeros((D_MODEL, LANE - N_GROUPS - N_EXPERTS), F32)
    fw['wr'] = jnp.concatenate([p['moe_w_group'][l], wrt, pad], axis=1).astype(BF16)
    fw['br'] = jnp.concatenate([p['moe_b_group'][l], p['moe_b_router'][l].reshape(-1),
                                jnp.zeros((LANE - N_GROUPS - N_EXPERTS,), F32)]).reshape(1, LANE)
    fw['w1'] = p['moe_w1'][l].astype(BF16)
    fw['w3'] = p['moe_w3'][l].astype(BF16)
    fw['w2'] = p['moe_w2'][l].astype(BF16)
    return fw


def _ffn(x, nsa, mla, dif, fw, *, tm):
    t = x.shape[0]
    assert t % tm == 0
    tok = lambda w: pl.BlockSpec((tm, w), lambda i, e: (i, 0))
    cst = lambda a: pl.BlockSpec(tuple(a.shape), lambda i, e: (0,) * a.ndim)
    won, wom, wod = fw['won'].astype(BF16), fw['wom'].astype(BF16), fw['wod'].astype(BF16)
    in_specs = [tok(D_MODEL), tok(768), tok(384), tok(512), cst(won), cst(wom), cst(wod),
                cst(fw['fnorm']), cst(fw['wr']), cst(fw['br']),
                pl.BlockSpec((1, D_MODEL, D_EXPERT), lambda i, e: (e, 0, 0)),
                pl.BlockSpec((1, D_MODEL, D_EXPERT), lambda i, e: (e, 0, 0)),
                pl.BlockSpec((1, D_EXPERT, D_MODEL), lambda i, e: (e, 0, 0))]
    return pl.pallas_call(
        _ffn_kernel,
        grid=(t // tm, N_EXPERTS), in_specs=in_specs, out_specs=tok(D_MODEL),
        out_shape=jax.ShapeDtypeStruct((t, D_MODEL), F32),
        scratch_shapes=[pltpu.VMEM((tm, D_MODEL), F32), pltpu.VMEM((tm, D_MODEL), BF16),
                        pltpu.VMEM((tm, LANE), F32), pltpu.VMEM((tm, D_MODEL), F32)],
        compiler_params=pltpu.CompilerParams(dimension_semantics=("arbitrary", "arbitrary"),
                                             vmem_limit_bytes=VMEM_LIMIT),
        name="outproj_moe")(x, nsa, mla, dif, won, wom, wod, fw['fnorm'], fw['wr'], fw['br'],
                            fw['w1'], fw['w3'], fw['w2'])


PAGES_PER_STEP = 16
CHUNK = PAGES_PER_STEP * PAGE_SIZE
TAIL = 8
ROWS8 = 8
N_SEL_PAD = 384


def _bias_rows(dist_row, tab_ref, col0, nheads, rows_per_head):
    w = dist_row.shape[1]
    masks = [dist_row >= _BUCKET_LB[k] for k in range(1, REL_BUCKETS)]
    row = lax.broadcasted_iota(jnp.int32, (ROWS8, w), 0)
    out = jnp.zeros((ROWS8, w), F32)
    for h in range(nheads):
        b = jnp.full(dist_row.shape, tab_ref[col0 + h], F32)
        for k in range(1, REL_BUCKETS):
            b = jnp.where(masks[k - 1], tab_ref[k * _TAB_COLS + col0 + h], b)
        out = jnp.where(row // rows_per_head == h, b, out)
    return out


def _far_bias_rows(tab_ref, col0, nheads, rows_per_head, w):
    row = lax.broadcasted_iota(jnp.int32, (ROWS8, w), 0)
    out = jnp.zeros((ROWS8, w), F32)
    for h in range(nheads):
        out = jnp.where(row // rows_per_head == h, tab_ref[(REL_BUCKETS - 1) * _TAB_COLS + col0 + h], out)
    return out


def _sample_kernel(*refs):
    np_ = PAGES_PER_STEP
    (pt_ref, tab_ref) = refs[0:2]
    (qn8_ref, qg8_ref, qr8_ref, qd8_ref, gates8_ref, tnsa_ref, tmla_ref, tdiff_ref, twin_ref,
     winbuf_ref) = refs[2:12]
    nsa_pages = refs[12:12 + np_]
    mla_pages = refs[12 + np_:12 + 2 * np_]
    diff_pages = refs[12 + 2 * np_:12 + 3 * np_]
    (wukt_ref, ngain_ref, wuv_ref, pem_ref, wc_ref, cgain_ref, pair_ref, dpar_ref) = refs[12 + 3 * np_:20 + 3 * np_]
    (onsa_ref, omla_ref, odiff_ref) = refs[20 + 3 * np_:23 + 3 * np_]
    (ut_sc, mm_sc, lm_sc, accm_sc, sd_sc, vd_sc, pooled_sc, ksvs_sc, selbuf_sc, bias_sc) = refs[23 + 3 * np_:]
    del pt_ref
    c = pl.program_id(1)
    nchunk = pl.num_programs(1)
    qpos = PAST_LEN

    @pl.when(c == 0)
    def _():
        mm_sc[...] = jnp.full(mm_sc.shape, NEG, F32)
        lm_sc[...] = jnp.zeros(lm_sc.shape, F32)
        accm_sc[...] = jnp.zeros(accm_sc.shape, F32)
        ut_sc[...] = jnp.dot(qg8_ref[0] * ngain_ref[...], wukt_ref[0:MLA_HEADS * MLA_NOPE, :],
                             precision=HIGHEST, preferred_element_type=F32)
        ksvs_sc[pl.ds(PAST_LEN, SEL_BLOCK), :] = jnp.zeros((SEL_BLOCK, LANE), F32)

    def mla_scores(xt):
        r = xt.shape[1]
        c_lat = xt[0:MLA_KV_LORA, :].astype(BF16)
        kkt = _mm(wukt_ref[...], c_lat)
        ssq = jnp.sum((kkt * kkt).reshape(ROWS8, MLA_NOPE, r), axis=1)
        num = _mm(ut_sc[...], c_lat)
        rope = _mm(qr8_ref[0], xt[MLA_KV_LORA:MLA_KV_LORA + MLA_ROPE, :])
        return (num * lax.rsqrt(ssq * (1.0 / MLA_NOPE) + RMS_EPS) + rope) * MLA_SCALE, c_lat

    def mla_accumulate(s, c_lat, keep):
        if keep is not None:
            s = jnp.where(keep, s, NEG)
        m_old = mm_sc[...]
        m_new = jnp.maximum(m_old, jnp.max(s, axis=-1, keepdims=True))
        alpha = jnp.exp(m_old - m_new)
        p = jnp.exp(s - m_new)
        if keep is not None:
            p = jnp.where(keep, p, 0.0)
        lm_sc[...] = alpha * lm_sc[...] + jnp.sum(p, axis=-1, keepdims=True)
        mm_sc[...] = m_new
        accm_sc[...] = alpha * accm_sc[...] + _mm_nt(p, c_lat)

    def diff_scores(x, bias):
        return _mm_nt(qd8_ref[0], x[:, 0:LANE]) * DIFF_SCALE + bias

    xn = jnp.concatenate([r[0, 0] for r in nsa_pages], axis=0)
    xmt = jnp.concatenate([r[0, 0] for r in mla_pages], axis=1)
    xd = jnp.concatenate([r[0, 0] for r in diff_pages], axis=0)
    k0 = c * CHUNK
    near = k0 + CHUNK - 1 > qpos - FAR_DIST
    dist_row = qpos - (k0 + lax.broadcasted_iota(jnp.int32, (1, CHUNK), 1))

    @pl.when(near)
    def _():
        bias_sc[...] = _bias_rows(dist_row, tab_ref, NSA_HEADS, DIFF_HEADS, 2)

    @pl.when(jnp.logical_not(near))
    def _():
        bias_sc[...] = _far_bias_rows(tab_ref, NSA_HEADS, DIFF_HEADS, 2, CHUNK)

    s_m, c_lat = mla_scores(xmt)
    lane_off = pl.multiple_of(k0, CHUNK)
    sd_sc[:, pl.ds(lane_off, CHUNK)] = diff_scores(xd, bias_sc[...])
    vd_sc[pl.ds(lane_off, CHUNK), :] = xd[:, LANE:2 * LANE].astype(BF16)
    mla_accumulate(s_m, c_lat, None)

    nblk = CHUNK // CMP_BLOCK
    pooled = jnp.sum(xn[:, 0:LANE].reshape(nblk, CMP_BLOCK, LANE), axis=1) * (1.0 / CMP_BLOCK)
    pooled_sc[pl.ds(pl.multiple_of(c * nblk, nblk), nblk), :] = pooled
    ksvs_sc[pl.ds(pl.multiple_of(k0, CHUNK), CHUNK), :] = xn[:, LANE:2 * LANE]

    @pl.when(c == nchunk - 1)
    def _():
        tail_keep = lax.broadcasted_iota(jnp.int32, (ROWS8, TAIL), 1) == 0
        s_t, c_t = mla_scores(tmla_ref[0])
        mla_accumulate(s_t, c_t, lax.broadcasted_iota(jnp.int32, (ROWS8, LANE), 1) == 0)
        lane_keep = lax.broadcasted_iota(jnp.int32, (ROWS8, LANE), 1) == 0
        xt_d = jnp.concatenate([tdiff_ref[0], jnp.zeros((LANE - TAIL, 2 * LANE), F32)], axis=0)
        s_td = diff_scores(xt_d, _bias_rows(jnp.zeros((1, LANE), jnp.int32), tab_ref, NSA_HEADS, DIFF_HEADS, 2))
        sd_sc[:, PAST_LEN:PAST_LEN + LANE] = jnp.where(lane_keep, s_td, NEG)
        vd_sc[PAST_LEN:PAST_LEN + LANE, :] = xt_d[:, LANE:2 * LANE].astype(BF16)
        ksvs_sc[pl.ds(PAST_LEN, TAIL), :] = tnsa_ref[0][:, LANE:2 * LANE]

        row8 = lax.broadcasted_iota(jnp.int32, (ROWS8, LANE), 0)
        lane8 = lax.broadcasted_iota(jnp.int32, (ROWS8, LANE), 1)
        lat = accm_sc[...] / lm_sc[...]
        o = jnp.dot(lat, wuv_ref[...], precision=HIGHEST, preferred_element_type=F32)
        r384 = lax.broadcasted_iota(jnp.int32, o.shape, 0)
        l384 = lax.broadcasted_iota(jnp.int32, o.shape, 1)
        omla_ref[0] = jnp.sum(jnp.where(l384 // MLA_V == r384, o, 0.0), axis=0, keepdims=True)

        s_all = sd_sc[...]
        e_d = jnp.exp(s_all - jnp.max(s_all, axis=-1, keepdims=True))
        p_d = e_d / jnp.sum(e_d, axis=-1, keepdims=True)
        a_d = p_d - dpar_ref[1:2, 0:1] * pltpu.roll(p_d, ROWS8 - 1, 0)
        a = jnp.dot(a_d.astype(BF16), vd_sc[...], preferred_element_type=F32)
        grp = row8 // 4
        a = jnp.where((lane8 >= grp * DIFF_V) & (lane8 < (grp + 1) * DIFF_V), a, 0.0)
        ms = jnp.sum(a * a, axis=-1, keepdims=True) * (1.0 / DIFF_V)
        odiff_ref[0] = a * lax.rsqrt(ms + RMS_EPS) * dpar_ref[0:1, :]

        qn8 = qn8_ref[0]
        nc = pooled_sc.shape[0]
        cm = _mm(pooled_sc[...] + pem_ref[...], wc_ref[...])
        lane_c = lax.broadcasted_iota(jnp.int32, cm.shape, 1)
        is_k = lane_c < HEAD_DIM
        msc = jnp.sum(jnp.where(is_k, cm * cm, 0.0), axis=-1, keepdims=True) * (1.0 / HEAD_DIM)
        cm = jnp.where(is_k, cm * lax.rsqrt(msc + RMS_EPS) * cgain_ref[...], cm)
        dist_c = qpos - ((lax.broadcasted_iota(jnp.int32, (1, nc), 1) + 1) * CMP_BLOCK - 1)
        s_c = _mm_nt(qn8, cm) * NSA_SCALE + _bias_rows(dist_c, tab_ref, 0, NSA_HEADS, 1)
        p_c = _masked_softmax_rows(s_c, dist_c >= 0)
        o_c = jnp.dot(p_c.astype(BF16), cm.astype(BF16), preferred_element_type=F32)
        rowc = lax.broadcasted_iota(jnp.int32, p_c.shape, 0)
        imp = jnp.sum(jnp.where(rowc < NSA_HEADS, p_c, 0.0), axis=0, keepdims=True)
        imp = jnp.dot(imp, pair_ref[...], precision=HIGHEST, preferred_element_type=F32)

        blk = lax.broadcasted_iota(jnp.int32, (1, N_SEL_PAD), 1)
        cur = qpos // SEL_BLOCK
        forced = (blk == 0) | (blk == cur) | (blk == cur - 1)
        score = jnp.where(blk * SEL_BLOCK <= qpos, jnp.where(forced, SEL_FORCED, imp), SEL_INVALID)
        lane_s = lax.broadcasted_iota(jnp.int32, (1, TOP_N * SEL_BLOCK), 1)
        pos = jnp.zeros((1, TOP_N * SEL_BLOCK), jnp.int32)
        for t in range(TOP_N):
            top = jnp.max(score)
            idx = jnp.min(jnp.where(score == top, blk, 1 << 20))
            score = jnp.where(blk == idx, SEL_TAKEN, score)
            selbuf_sc[t * SEL_BLOCK:(t + 1) * SEL_BLOCK, :] = ksvs_sc[pl.ds(pl.multiple_of(idx * SEL_BLOCK, SEL_BLOCK), SEL_BLOCK), :]
            pos = jnp.where(lane_s // SEL_BLOCK == t, idx * SEL_BLOCK + lane_s % SEL_BLOCK, pos)
        sb = selbuf_sc[...]
        dist_s = qpos - pos
        s_s = _mm_nt(qn8, sb) * NSA_SCALE + _bias_rows(dist_s, tab_ref, 0, NSA_HEADS, 1)
        p_s = _masked_softmax_rows(s_s, dist_s >= 0)
        o_s = jnp.dot(p_s.astype(BF16), sb.astype(BF16), preferred_element_type=F32)

        wb = winbuf_ref[0]
        nw = wb.shape[0]
        tw = twin_ref[0]
        dist_w = nw - lax.broadcasted_iota(jnp.int32, (1, nw), 1)
        keep_w = dist_w < WINDOW
        s_w = jnp.where(keep_w, _mm_nt(qn8, wb) * NSA_SCALE
                        + _bias_rows(dist_w, tab_ref, 0, NSA_HEADS, 1), NEG)
        s_n = jnp.where(tail_keep, _mm_nt(qn8, tw) * NSA_SCALE
                        + _bias_rows(jnp.zeros((1, TAIL), jnp.int32), tab_ref, 0, NSA_HEADS, 1), NEG)
        m_w = jnp.maximum(jnp.max(s_w, axis=-1, keepdims=True), jnp.max(s_n, axis=-1, keepdims=True))
        e_w = jnp.where(keep_w, jnp.exp(s_w - m_w), 0.0)
        e_n = jnp.where(tail_keep, jnp.exp(s_n - m_w), 0.0)
        z_w = jnp.sum(e_w, axis=-1, keepdims=True) + jnp.sum(e_n, axis=-1, keepdims=True)
        o_w = _mm(e_w / z_w, wb) + _mm(e_n / z_w, tw)

        g = gates8_ref[0]
        onsa_ref[0] = g[:, 0:1] * o_c + g[:, 1:2] * o_s + g[:, 2:3] * o_w


def _sample_weights(l, p):
    wukv = p['mla_w_ukv'][l]
    cols_k = np.concatenate([h * (MLA_NOPE + MLA_V) + np.arange(MLA_NOPE) for h in range(MLA_HEADS)])
    cols_v = np.concatenate([h * (MLA_NOPE + MLA_V) + MLA_NOPE + np.arange(MLA_V) for h in range(MLA_HEADS)])
    wukt = jnp.transpose(jnp.take(wukv, jnp.asarray(cols_k), axis=1))
    sw = {}
    sw['wukt'] = jnp.concatenate([wukt, jnp.zeros((ROWS8 * MLA_NOPE - wukt.shape[0], MLA_KV_LORA), F32)], axis=0)
    sw['ngain'] = jnp.tile(p['mla_nope_gain'][l][1], MLA_HEADS).reshape(1, -1)
    sw['wuv'] = jnp.take(wukv, jnp.asarray(cols_v), axis=1)
    return sw


def _sample_mixers(l, page_table, cache_nsa, cache_mla, cache_diff, win_buf, pr, lw, sw, table, dpar):
    nb = page_table.shape[0]
    npages = page_table.shape[1]
    assert npages * PAGE_SIZE == PAST_LEN and npages % PAGES_PER_STEP == 0
    nchunk = npages // PAGES_PER_STEP
    pad_rows = lambda a, n: jnp.concatenate([a, jnp.zeros((a.shape[0], n - a.shape[1], a.shape[2]), a.dtype)], axis=1)
    qn8 = pad_rows(pr['qnsa'].reshape(nb, NSA_HEADS, LANE), ROWS8)
    qm = pr['qmla'].reshape(nb, MLA_HEADS, LANE)
    eye = jnp.eye(ROWS8, MLA_HEADS, dtype=F32)
    qg8 = (eye[None, :, :, None] * qm[:, None, :, :MLA_NOPE]).reshape(nb, ROWS8, MLA_HEADS * MLA_NOPE)
    qr8 = pad_rows(qm[:, :, MLA_NOPE:MLA_NOPE + MLA_ROPE], ROWS8)
    qd8 = pr['qd'].reshape(nb, N_DMAPS, LANE)
    gates8 = pad_rows(jnp.concatenate([pr['gates'][:, :3 * NSA_HEADS].reshape(nb, NSA_HEADS, 3),
                                       jnp.zeros((nb, NSA_HEADS, LANE - 3), F32)], axis=2), ROWS8)
    mla_row = jnp.concatenate([pr['ckv'], pr['kr'][:, ROPE_LO:ROPE_LO + MLA_ROPE]], axis=1)
    tail = lambda a: pad_rows(a[:, None, :], TAIL)
    tnsa, tdiff, twin = tail(pr['nsarow']), tail(pr['diffrow']), tail(pr['winrow'])
    tmla = jnp.concatenate([mla_row[:, :, None], jnp.zeros((nb, mla_row.shape[1], LANE - 1), F32)], axis=2)
    cache_mla_t = jnp.swapaxes(cache_mla, 2, 3)
    pair = np.zeros((PAST_LEN // CMP_BLOCK, N_SEL_PAD), np.float32)
    ncb = PAST_LEN // CMP_BLOCK
    pair[np.arange(ncb), np.arange(ncb) // (SEL_BLOCK // CMP_BLOCK)] = 1.0

    seq = lambda a: pl.BlockSpec((1,) + tuple(a.shape[1:]), lambda b, c, pt: (b,) + (0,) * (a.ndim - 1))
    cst = lambda a: pl.BlockSpec(tuple(np.shape(a)), lambda b, c, pt: (0,) * np.ndim(a))

    def page_spec(cache, i):
        blk = (1, 1) + tuple(cache.shape[2:])
        return pl.BlockSpec(blk, lambda b, c, pt: (l, pt[b, c * PAGES_PER_STEP + i], 0, 0))

    seq_ins = [qn8, qg8, qr8, qd8, gates8, tnsa, tmla, tdiff, twin, win_buf]
    consts = [sw['wukt'], sw['ngain'], sw['wuv'], lw['pe_mean'], lw['wc'], lw['cgain'], pair, dpar]
    pages = [cache_nsa] * PAGES_PER_STEP + [cache_mla_t] * PAGES_PER_STEP + [cache_diff] * PAGES_PER_STEP
    in_specs = [pl.BlockSpec(memory_space=pltpu.SMEM)] + [seq(a) for a in seq_ins] \
        + [page_spec(cache_nsa, i) for i in range(PAGES_PER_STEP)] \
        + [page_spec(cache_mla_t, i) for i in range(PAGES_PER_STEP)] \
        + [page_spec(cache_diff, i) for i in range(PAGES_PER_STEP)] \
        + [cst(a) for a in consts]
    out_shape = [jax.ShapeDtypeStruct((nb, ROWS8, LANE), F32), jax.ShapeDtypeStruct((nb, 1, MLA_HEADS * MLA_V), F32),
                 jax.ShapeDtypeStruct((nb, ROWS8, LANE), F32)]
    out_specs = [pl.BlockSpec((1, ROWS8, LANE), lambda b, c, pt: (b, 0, 0)),
                 pl.BlockSpec((1, 1, MLA_HEADS * MLA_V), lambda b, c, pt: (b, 0, 0)),
                 pl.BlockSpec((1, ROWS8, LANE), lambda b, c, pt: (b, 0, 0))]
    r8 = (ROWS8, 1)
    scratch = [pltpu.VMEM((ROWS8, MLA_KV_LORA), F32),
               pltpu.VMEM(r8, F32), pltpu.VMEM(r8, F32), pltpu.VMEM((ROWS8, LANE), F32),
               pltpu.VMEM((ROWS8, PAST_LEN + LANE), F32), pltpu.VMEM((PAST_LEN + LANE, LANE), BF16),
               pltpu.VMEM((PAST_LEN // CMP_BLOCK, LANE), F32),
               pltpu.VMEM((PAST_LEN + SEL_BLOCK, LANE), F32),
               pltpu.VMEM((TOP_N * SEL_BLOCK, LANE), F32),
               pltpu.VMEM((ROWS8, CHUNK), F32)]
    grid_spec = pltpu.PrefetchScalarGridSpec(num_scalar_prefetch=1, grid=(nb, nchunk), in_specs=in_specs,
                                             out_specs=out_specs, scratch_shapes=scratch)
    onsa, omla, odiff = pl.pallas_call(
        _sample_kernel, grid_spec=grid_spec, out_shape=out_shape,
        compiler_params=pltpu.CompilerParams(dimension_semantics=("arbitrary", "arbitrary"),
                                             vmem_limit_bytes=VMEM_LIMIT),
        name="sample_paged_mixers")(page_table, table.reshape(-1), *seq_ins, *pages, *consts)
    nsa = onsa[:, :NSA_HEADS].reshape(nb, NSA_HEADS * LANE)
    dif = odiff[:, 0::2].reshape(nb, DIFF_HEADS * LANE)
    return nsa, omla[:, 0], dif, mla_row


PROMPT_TM = 256
PROMPT_FFN_TM = 512


def kernel(x_prompt, x_sample, cache_nsa, cache_mla, cache_diff, state_nsa_win, page_table, attn_norm, w_in, nsa_qk_gain, nsa_cmp_pe, nsa_cmp_w, mla_q_norm, mla_kv_norm, mla_w_uq, mla_w_ukv, mla_nope_gain, mla_rope_gain, diff_qk_gain, diff_lambda, diff_head_gain, w_out, rel_bias_table, ffn_norm, moe_w_group, moe_b_group, moe_w_router, moe_b_router, moe_w1, moe_w3, moe_w2):
    p = dict(attn_norm=attn_norm, w_in=w_in, nsa_qk_gain=nsa_qk_gain, nsa_cmp_pe=nsa_cmp_pe,
             nsa_cmp_w=nsa_cmp_w, mla_q_norm=mla_q_norm, mla_kv_norm=mla_kv_norm, mla_w_uq=mla_w_uq,
             mla_w_ukv=mla_w_ukv, mla_nope_gain=mla_nope_gain, mla_rope_gain=mla_rope_gain,
             diff_qk_gain=diff_qk_gain, diff_lambda=diff_lambda, diff_head_gain=diff_head_gain,
             w_out=w_out, ffn_norm=ffn_norm, moe_w_group=moe_w_group, moe_b_group=moe_b_group,
             moe_w_router=moe_w_router, moe_b_router=moe_b_router, moe_w1=moe_w1, moe_w3=moe_w3,
             moe_w2=moe_w2)
    assert x_prompt.shape[0] == 1 and x_sample.shape[1] == 1
    xp, xs = x_prompt[0], x_sample[:, 0]
    s, nb = xp.shape[0], xs.shape[0]
    pos_p = jnp.arange(s)
    pos_s = jnp.full((nb,), PAST_LEN, jnp.int32)
    table = rel_bias_table.astype(F32)
    wb = state_nsa_win.shape[2]
    outs = [[] for _ in range(8)]
    for l in range(w_in.shape[0]):
        lw, fw, sw, dpar = _layer_weights(l, p), _ffn_weights(l, p), _sample_weights(l, p), _diff_params(l, p)
        pp = _project(xp, pos_p, lw, tm=PROMPT_TM, fold_scale=True, qdtype=BF16)
        nsa = _nsa_attention(pp['qnsa'], pp['gates'], pp['nsarowb'], pp['winrowb'], pp['cmp'], table)
        mla = _mla_attention(pp['qmla'], pp['kmla'], pp['vmla'])
        dif = _diff_attention(pp['qd'], pp['diffrowb'], table, dpar)
        xp = _ffn(xp, nsa, mla, dif, fw, tm=PROMPT_FFN_TM)
        ps = _project(xs, pos_s, lw, tm=nb, fold_scale=False, qdtype=F32)
        win_buf = state_nsa_win[l]
        nsa_s, mla_s, dif_s, mla_row_s = _sample_mixers(l, page_table, cache_nsa, cache_mla, cache_diff,
                                                        win_buf, ps, lw, sw, table, dpar)
        xs = _ffn(xs, nsa_s, mla_s, dif_s, fw, tm=nb)
        mla_row_p = jnp.concatenate([pp['ckv'], pp['kr'][:, ROPE_LO:ROPE_LO + MLA_ROPE]], axis=1)
        new = (pp['nsarow'][None], ps['nsarow'][:, None], mla_row_p[None], mla_row_s[:, None],
               pp['diffrow'][None], ps['diffrow'][:, None], pp['winrow'][None, s - min(WINDOW, s):],
               jnp.concatenate([win_buf, ps['winrow'][:, None]], axis=1)[:, -wb:])
        for acc, a in zip(outs, new):
            acc.append(a)
    return (xp[None], xs[:, None]) + tuple(jnp.stack(a) for a in outs)
```
